```python
import math
import jax, jax.numpy as jnp
from jax import lax
import numpy as np

D_MODEL = 1024
BATCH = 8
SEQ = 4096
DEPTH = 1

HEAD_DIM = 64
N_ATTN_HEADS = 8
N_GMLP_GROUPS = 8
D_ATTN = N_ATTN_HEADS * HEAD_DIM
D_GMLP = N_GMLP_GROUPS * HEAD_DIM
D_MIX = D_ATTN + D_GMLP
D_IN = 3 * D_ATTN + 2 * D_GMLP
DILATIONS = ((128, 1), (512, 4), (2048, 16))
ROPE_THETA = 500000.0
ROPE_DIM = HEAD_DIM // 4
CHUNK = 128
D_FF = 2816
CONV_WIDTH = 3
D_PLE = 256
LN_EPS = 1e-5
ALPHA = (2.0 * DEPTH) ** 0.25
BETA = (8.0 * DEPTH) ** -0.25
NEG_INF = -1e30

kernel_name = "hybrid_dilated_attn_gmlp_deepnorm_layer"


def _layernorm(x, g, b):
    xf = x.astype(jnp.float32)
    mu = jnp.mean(xf, axis=-1, keepdims=True)
    var = jnp.mean(jnp.square(xf - mu), axis=-1, keepdims=True)
    y = (xf - mu) * lax.rsqrt(var + LN_EPS)
    return (y * g.astype(jnp.float32) + b.astype(jnp.float32)).astype(x.dtype)


def _partial_rope(t, positions):
    inv = ROPE_THETA ** (-jnp.arange(0, ROPE_DIM, 2, dtype=jnp.float32) / ROPE_DIM)
    ang = positions.astype(jnp.float32)[..., None] * inv
    cos = jnp.cos(ang)[:, :, None, :]
    sin = jnp.sin(ang)[:, :, None, :]
    half = ROPE_DIM // 2
    x1 = t[..., :half]
    x2 = t[..., half:ROPE_DIM]
    rot = jnp.concatenate([x1 * cos - x2 * sin, x2 * cos + x1 * sin], axis=-1)
    return jnp.concatenate([rot, t[..., ROPE_DIM:]], axis=-1)


def _dilated_branch(q, k, v, window, dilation):
    b, s, h, dh = q.shape
    d = dilation
    w = window // d
    L = s // d
    nb = -(-L // w)
    Lp = nb * w

    def sub(t):
        t = t.reshape(b, L, d, h, dh)
        return jnp.pad(t, ((0, 0), (0, Lp - L), (0, 0), (0, 0), (0, 0)))

    def band(t):
        tp = jnp.pad(t, ((0, 0), (w, 0), (0, 0), (0, 0), (0, 0)))
        prev = tp[:, :Lp].reshape(b, nb, w, d, h, dh)
        cur = t.reshape(b, nb, w, d, h, dh)
        return jnp.concatenate([prev, cur], axis=2)

    qb = sub(q).reshape(b, nb, w, d, h, dh)
    kb = band(sub(k))
    vb = band(sub(v))
    scores = jnp.einsum('bnqrhd,bnkrhd->bnrhqk', qb, kb) * (1.0 / math.sqrt(dh))
    qi = jnp.arange(nb)[:, None, None] * w + jnp.arange(w)[None, :, None]
    kj = jnp.arange(nb)[:, None, None] * w - w + jnp.arange(2 * w)[None, None, :]
    dist = qi - kj
    mask = (dist >= 0) & (dist <= w) & (kj >= 0)
    scores = jnp.where(mask[None, :, None, None], scores, NEG_INF)
    m = jnp.max(scores, axis=-1, keepdims=True)
    e = jnp.exp(scores - m)
    den = jnp.sum(e, axis=-1, keepdims=True)
    out = jnp.einsum('bnrhqk,bnkrhd->bnqrhd', e / den, vb)
    lse = jnp.transpose((m + jnp.log(den))[..., 0], (0, 1, 4, 2, 3))
    out = out.reshape(b, Lp, d, h, dh)[:, :L].reshape(b, s, h, dh)
    lse = lse.reshape(b, Lp, d, h)[:, :L].reshape(b, s, h)
    return out, lse


def _dilated_attention(q, k, v):
    outs, lses = [], []
    for window, dilation in DILATIONS:
        o, l = _dilated_branch(q, k, v, window, dilation)
        outs.append(o)
        lses.append(l)
    wts = jax.nn.softmax(jnp.stack(lses, axis=0), axis=0)
    return jnp.einsum('cbsh,cbshd->bshd', wts, jnp.stack(outs, axis=0))


def _chunked_gmlp(u, z, ln_z_g, ln_z_b, w_s, b_s):
    b, s, _ = z.shape
    zn = _layernorm(z, ln_z_g, ln_z_b)
    zc = zn.reshape(b, s // CHUNK, CHUNK, N_GMLP_GROUPS, HEAD_DIM)
    causal = jnp.tril(jnp.ones((CHUNK, CHUNK), dtype=w_s.dtype))
    mixed = jnp.einsum('gij,bcjgd->bcigd', w_s * causal, zc) + jnp.transpose(b_s)[None, None, :, :, None]
    return u * mixed.reshape(b, s, D_GMLP)


def _causal_dwconv(a, w, bias):
    s = a.shape[1]
    ap = jnp.pad(a, ((0, 0), (CONV_WIDTH - 1, 0), (0, 0)))
    out = bias
    for kk in range(CONV_WIDTH):
        out = out + w[kk] * ap[:, kk:kk + s]
    return out


def setup_inputs(seed: int = 0) -> dict:
    key = jax.random.key(seed)
    ks = jax.random.split(key, 24)
    nrm = lambda k, shape, scale: jax.random.normal(k, shape, dtype=jnp.float32) * scale
    gain = lambda k, n: 1.0 + nrm(k, (DEPTH, n), 0.01)
    bias = lambda k, n: nrm(k, (DEPTH, n), 0.01)
    x = nrm(ks[0], (BATCH, SEQ, D_MODEL), 1.0)
    p = nrm(ks[1], (DEPTH, BATCH, SEQ, D_PLE), 1.0)
    offs = jax.random.randint(ks[2], (BATCH, 1), 0, 1024, dtype=jnp.int32)
    positions = offs + jnp.arange(SEQ, dtype=jnp.int32)[None, :]
    w_in = nrm(ks[3], (DEPTH, D_MODEL, D_IN), D_MODEL ** -0.5)
    col_scale = jnp.concatenate([jnp.ones((2 * D_ATTN,), jnp.float32),
                                 jnp.full((D_ATTN,), BETA, jnp.float32),
                                 jnp.ones((2 * D_GMLP,), jnp.float32)])
    w_in = w_in * col_scale
    return {
        "x": x,
        "p": p,
        "positions": positions,
        "w_in": w_in,
        "ln_z_g": gain(ks[4], D_GMLP),
        "ln_z_b": bias(ks[5], D_GMLP),
        "w_s": nrm(ks[6], (DEPTH, N_GMLP_GROUPS, CHUNK, CHUNK), CHUNK ** -0.5),
        "b_s": 1.0 + nrm(ks[7], (DEPTH, N_GMLP_GROUPS, CHUNK), 0.01),
        "w_o": nrm(ks[8], (DEPTH, D_MIX, D_MODEL), BETA * D_MIX ** -0.5),
        "ln1_g": gain(ks[9], D_MODEL),
        "ln1_b": bias(ks[10], D_MODEL),
        "w_ff_a": nrm(ks[11], (DEPTH, D_MODEL, D_FF), BETA * D_MODEL ** -0.5),
        "w_ff_b": nrm(ks[12], (DEPTH, D_MODEL, D_FF), BETA * D_MODEL ** -0.5),
        "conv_w": nrm(ks[13], (DEPTH, CONV_WIDTH, D_FF), CONV_WIDTH ** -0.5),
        "conv_b": bias(ks[14], D_FF),
        "w_ff_down": nrm(ks[15], (DEPTH, D_FF, D_MODEL), BETA * D_FF ** -0.5),
        "ln2_g": gain(ks[16], D_MODEL),
        "ln2_b": bias(ks[17], D_MODEL),
        "w_ple_gate": nrm(ks[18], (DEPTH, D_MODEL, D_MODEL), D_MODEL ** -0.5),
        "b_ple_gate": bias(ks[19], D_MODEL),
        "w_ple_in": nrm(ks[20], (DEPTH, D_PLE, D_MODEL), BETA * D_PLE ** -0.5),
        "ln3_g": gain(ks[21], D_MODEL),
        "ln3_b": bias(ks[22], D_MODEL),
    }


def reference(x, p, positions, w_in, ln_z_g, ln_z_b, w_s, b_s, w_o, ln1_g, ln1_b,
              w_ff_a, w_ff_b, conv_w, conv_b, w_ff_down, ln2_g, ln2_b,
              w_ple_gate, b_ple_gate, w_ple_in, ln3_g, ln3_b):
    b, s, _ = x.shape
    for i in range(DEPTH):
        h = x @ w_in[i]
        q = h[..., :D_ATTN].reshape(b, s, N_ATTN_HEADS, HEAD_DIM)
        k = h[..., D_ATTN:2 * D_ATTN].reshape(b, s, N_ATTN_HEADS, HEAD_DIM)
        v = h[..., 2 * D_ATTN:3 * D_ATTN].reshape(b, s, N_ATTN_HEADS, HEAD_DIM)
        u = jax.nn.gelu(h[..., 3 * D_ATTN:3 * D_ATTN + D_GMLP], approximate=False)
        z = jax.nn.gelu(h[..., 3 * D_ATTN + D_GMLP:], approximate=False)
        q = _partial_rope(q, positions).astype(jnp.float32)
        k = _partial_rope(k, positions).astype(jnp.float32)
        attn = _dilated_attention(q, k, v.astype(jnp.float32)).astype(x.dtype).reshape(b, s, D_ATTN)
        gm = _chunked_gmlp(u, z, ln_z_g[i], ln_z_b[i], w_s[i], b_s[i])
        mix = jnp.concatenate([attn, gm], axis=-1) @ w_o[i]
        x = _layernorm(ALPHA * x + mix, ln1_g[i], ln1_b[i])
        a = _causal_dwconv(x @ w_ff_a[i], conv_w[i], conv_b[i])
        ff = (jax.nn.gelu(a, approximate=False) * (x @ w_ff_b[i])) @ w_ff_down[i]
        x = _layernorm(ALPHA * x + ff, ln2_g[i], ln2_b[i])
        gate = jax.nn.sigmoid(x @ w_ple_gate[i] + b_ple_gate[i])
        ple = gate * (p[i] @ w_ple_in[i])
        x = _layernorm(ALPHA * x + ple, ln3_g[i], ln3_b[i])
    return x
```

```python
import functools
import math

import numpy as np
import jax
import jax.numpy as jnp
from jax import lax
from jax.experimental import pallas as pl
from jax.experimental.pallas import tpu as pltpu

HEAD_DIM = 64
N_HEADS = 8
D_ATTN = 512
D_GMLP = 512
N_GROUPS = 8
CHUNK = 128
ROPE_THETA = 500000.0
ROPE_DIM = 16
LN_EPS = 1e-5
NEG_INF = -1e30
WINDOW_BLOCK = 128
LANES = 128
N_PAIRS = D_ATTN // LANES
N_RES = 16
VMEM_LIMIT = 56 * 1024 * 1024

_INV_SQRT2 = 0.7071067811865476


def _gelu(t):
    return 0.5 * t * (1.0 + lax.erf(t * _INV_SQRT2))


def _layernorm_rows(t, g, b):
    mu = jnp.mean(t, axis=-1, keepdims=True)
    d = t - mu
    var = jnp.mean(d * d, axis=-1, keepdims=True)
    return d * lax.rsqrt(var + LN_EPS) * g + b


def _proj_kernel(x_ref, pos_ref, inv_ref, wqkv_ref, wuzT_ref, lnzg_ref, lnzb_ref, wsT_ref, bs_ref,
                 q_ref, k_ref, v_ref, gm_ref, nat_s, mid_s, *, tm):
    xb = x_ref[...].astype(jnp.bfloat16)
    h = jnp.dot(xb, wqkv_ref[...], preferred_element_type=jnp.float32)

    pos = pos_ref[...].astype(jnp.float32)
    ang = pos * inv_ref[...]
    cos_t = jnp.cos(ang)
    sin_t = jnp.sin(ang)
    ones48 = jnp.ones((HEAD_DIM - ROPE_DIM, tm), jnp.float32)
    zeros48 = jnp.zeros((HEAD_DIM - ROPE_DIM, tm), jnp.float32)
    c_t = jnp.concatenate([cos_t, cos_t, ones48, cos_t, cos_t, ones48], axis=0)
    s_t = jnp.concatenate([-sin_t, sin_t, zeros48, -sin_t, sin_t, zeros48], axis=0)
    c_tab = c_t.T
    s_tab = s_t.T
    lane = lax.broadcasted_iota(jnp.int32, (1, LANES), 1)
    first_half = (lane % HEAD_DIM) < (ROPE_DIM // 2)

    def rope(t):
        up = pltpu.roll(t, LANES - ROPE_DIM // 2, 1)
        dn = pltpu.roll(t, ROPE_DIM // 2, 1)
        return t * c_tab + jnp.where(first_half, up, dn) * s_tab

    scale = 1.0 / math.sqrt(HEAD_DIM)
    for hp in range(N_PAIRS):
        lo = hp * LANES
        nat_s[hp] = rope(h[:, lo:lo + LANES]) * scale
        nat_s[N_PAIRS + hp] = rope(h[:, D_ATTN + lo:D_ATTN + lo + LANES])
        nat_s[2 * N_PAIRS + hp] = h[:, 2 * D_ATTN + lo:2 * D_ATTN + lo + LANES]

    quarter = tm // 4
    n_l = tm // N_RES
    for idx in range(3 * N_PAIRS):
        for r in range(4):
            mid_s[idx, r * quarter:(r + 1) * quarter, :] = nat_s[idx, pl.ds(r, quarter, stride=4), :]
    for which, o_ref in enumerate((q_ref, k_ref, v_ref)):
        for hp in range(N_PAIRS):
            idx = which * N_PAIRS + hp
            for r_lo in range(4):
                for r_hi in range(4):
                    o_ref[hp, 4 * r_hi + r_lo] = mid_s[idx, pl.ds(r_lo * quarter + r_hi, n_l, stride=4), :]

    h_t = lax.dot_general(wuzT_ref[...], xb, (((1,), (1,)), ((), ())),
                          preferred_element_type=jnp.float32)
    u_t = _gelu(h_t[:D_GMLP])
    z_t = _gelu(h_t[D_GMLP:])
    mu = jnp.mean(z_t, axis=0, keepdims=True)
    d = z_t - mu
    var = jnp.mean(d * d, axis=0, keepdims=True)
    zn_t = (d * lax.rsqrt(var + LN_EPS) * lnzg_ref[...] + lnzb_ref[...]).astype(jnp.bfloat16)

    n_chunks = tm // CHUNK
    row = lax.broadcasted_iota(jnp.int32, (CHUNK, CHUNK), 0)
    col = lax.broadcasted_iota(jnp.int32, (CHUNK, CHUNK), 1)
    keep = row <= col
    mixed = []
    for g in range(N_GROUPS):
        r0 = g * HEAD_DIM
        lhs = jnp.concatenate([zn_t[r0:r0 + HEAD_DIM, c * CHUNK:(c + 1) * CHUNK] for c in range(n_chunks)],
                              axis=0)
        w = jnp.where(keep, wsT_ref[g], 0.0).astype(jnp.bfloat16)
        res = jnp.dot(lhs, w, preferred_element_type=jnp.float32)
        res = res + bs_ref[g:g + 1, :]
        mixed.append(jnp.concatenate([res[c * HEAD_DIM:(c + 1) * HEAD_DIM] for c in range(n_chunks)], axis=1))
    mixed_t = jnp.concatenate(mixed, axis=0)
    gm_ref[...] = (u_t * mixed_t).T.astype(jnp.bfloat16)


def _branch_masks():
    w = WINDOW_BLOCK
    out = np.zeros((6, w, 2 * w), np.float32)
    rq, i8 = np.divmod(np.arange(w), 8)
    rk, j16 = np.divmod(np.arange(2 * w), 16)
    base = 16 * (i8[:, None] - j16[None, :]) + (rq[:, None] - rk[None, :])
    aq, i32 = np.divmod(np.arange(w), 32)
    ak, j64 = np.divmod(np.arange(2 * w), 64)
    base4 = 4 * (i32[:, None] - j64[None, :]) + (aq[:, None] - ak[None, :])
    base16 = np.arange(w)[:, None] - np.arange(2 * w)[None, :]
    for c, b in enumerate((base, base4, base16)):
        dist = b + w
        out[2 * c] = (dist >= 0) & (dist <= w)
        dist0 = b
        out[2 * c + 1] = (dist0 >= 0) & (dist0 <= w)
    return out


def _attn_kernel(q_s, k_s, v_s, mask_ref, o_ref, acc_s, m_s, l_s):
    n_l = q_s.shape[2]
    n_spans = n_l // WINDOW_BLOCK

    lane = lax.broadcasted_iota(jnp.int32, (1, LANES), 1)
    head0 = lane < HEAD_DIM

    def pair_block(hp, q_chunks, kv_chunks, st_chunks, mask, first, last, out_idx):
        def gather(ref, chunks):
            return jnp.concatenate([ref[hp, r, pl.ds(st, n), :] for (r, st, n) in chunks], axis=0)

        qv = gather(q_s, q_chunks)
        kb = gather(k_s, kv_chunks).astype(jnp.bfloat16)
        vb = gather(v_s, kv_chunks).astype(jnp.bfloat16)
        q2 = jnp.concatenate([jnp.where(head0, qv, 0.0), jnp.where(head0, 0.0, qv)],
                             axis=0).astype(jnp.bfloat16)
        s2 = lax.dot_general(q2, kb, (((1,), (1,)), ((), ())),
                             preferred_element_type=jnp.float32)
        valid = mask > 0.5
        w = WINDOW_BLOCK
        if not first:
            m_old = jnp.concatenate([m_s[hp, r, pl.ds(st, n), :] for (r, st, n) in st_chunks], axis=0)
            l_old = jnp.concatenate([l_s[hp, r, pl.ds(st, n), :] for (r, st, n) in st_chunks], axis=0)
            acc_old = jnp.concatenate([acc_s[hp, r, pl.ds(st, n), :] for (r, st, n) in st_chunks], axis=0)
        ps, ms, ls = [], [], []
        for hh in range(2):
            s = jnp.where(valid, s2[hh * w:(hh + 1) * w], NEG_INF)
            t = jnp.maximum(s[:, :LANES], s[:, LANES:])
            if not first:
                own = head0 if hh == 0 else jnp.logical_not(head0)
                t = jnp.maximum(t, jnp.where(own, m_old, NEG_INF))
            m_new = jnp.max(t, axis=1, keepdims=True)
            p = jnp.exp(s - m_new)
            ps.append(p)
            ms.append(m_new)
            ls.append(jnp.sum(p, axis=1, keepdims=True))
        p2 = jnp.concatenate(ps, axis=0).astype(jnp.bfloat16)
        o2 = jnp.dot(p2, vb, preferred_element_type=jnp.float32)
        o_pair = jnp.where(head0, o2[:w], o2[w:])
        m_b = jnp.where(head0, ms[0], ms[1])
        l_b = jnp.where(head0, ls[0], ls[1])
        if first:
            acc, l_tot = o_pair, l_b
        else:
            alpha = jnp.exp(m_old - m_b)
            acc = acc_old * alpha + o_pair
            l_tot = l_old * alpha + l_b
        if last:
            (r, st, n), = q_chunks
            o_ref[hp, r, pl.ds(st, n), :] = (acc / l_tot).astype(o_ref.dtype)
        else:
            off = 0
            for (r, st, n) in st_chunks:
                acc_s[hp, r, pl.ds(st, n), :] = acc[off:off + n]
                m_s[hp, r, pl.ds(st, n), :] = m_b[off:off + n]
                l_s[hp, r, pl.ds(st, n), :] = l_tot[off:off + n]
                off += n

    def for_pairs(fn):
        def body(hp, carry):
            fn(hp)
            return carry
        lax.fori_loop(0, N_PAIRS, body, 0)

    for span in range(n_spans):
        def d1_body(n, carry, span=span):
            nb = span * 16 + n
            q0 = pl.multiple_of(nb * 8, 8)
            k0 = pl.multiple_of(jnp.maximum(nb - 1, 0) * 8, 8)
            s0 = pl.multiple_of(n * 8, 8)
            mask = mask_ref[jnp.where(nb == 0, 1, 0)]
            for_pairs(lambda hp: pair_block(
                hp, [(r, q0, 8) for r in range(N_RES)], [(r, k0, 16) for r in range(N_RES)],
                [(r, s0, 8) for r in range(N_RES)], mask, True, False, None))
            return carry
        lax.fori_loop(0, 16, d1_body, 0)

        def d4_body(it, carry, span=span):
            r4 = it // 4
            n = span * 4 + it % 4
            q0 = pl.multiple_of(n * 32, 32)
            k0 = pl.multiple_of(jnp.maximum(n - 1, 0) * 32, 32)
            s0 = pl.multiple_of((it % 4) * 32, 32)
            mask = mask_ref[2 + jnp.where(n == 0, 1, 0)]
            for_pairs(lambda hp: pair_block(
                hp, [(r4 + 4 * a, q0, 32) for a in range(4)], [(r4 + 4 * a, k0, 64) for a in range(4)],
                [(r4 + 4 * a, s0, 32) for a in range(4)], mask, False, False, None))
            return carry
        lax.fori_loop(0, 16, d4_body, 0)

        def d16_body(r16, carry, span=span):
            q0 = span * WINDOW_BLOCK
            k0 = max(span - 1, 0) * WINDOW_BLOCK
            mask = mask_ref[4 + (1 if span == 0 else 0)]
            for_pairs(lambda hp: pair_block(
                hp, [(r16, q0, WINDOW_BLOCK)], [(r16, k0, 2 * WINDOW_BLOCK)],
                [(r16, 0, WINDOW_BLOCK)], mask, False, True, r16))
            return carry
        lax.fori_loop(0, N_RES, d16_body, 0)


def _tail_kernel(attn_ref, gm_ref, x_ref, p_ref, wo_ref, ln1g_ref, ln1b_ref, wa_ref, wb_ref, cw_ref, cb_ref,
                 wd_ref, ln2g_ref, ln2b_ref, wg_ref, bg_ref, wp_ref, ln3g_ref, ln3b_ref, o_ref, a_s, un_s,
                 *, tm, tiles_per_seq, alpha):
    i = pl.program_id(0)

    @pl.when(i % tiles_per_seq == 0)
    def _():
        a_s[0:8, :] = jnp.zeros((8, a_s.shape[1]), jnp.float32)

    n_l = tm // N_RES
    for hp in range(N_PAIRS):
        for r in range(N_RES):
            un_s[hp, pl.ds(r, n_l, stride=N_RES), :] = attn_ref[hp, r].astype(jnp.float32)
    mixin = jnp.concatenate([un_s[hp].astype(jnp.bfloat16) for hp in range(N_PAIRS)] + [gm_ref[...]],
                            axis=1)
    mix = jnp.dot(mixin, wo_ref[...], preferred_element_type=jnp.float32)
    x1 = _layernorm_rows(alpha * x_ref[...] + mix, ln1g_ref[...], ln1b_ref[...])
    x1b = x1.astype(jnp.bfloat16)

    a_s[8:8 + tm, :] = jnp.dot(x1b, wa_ref[...], preferred_element_type=jnp.float32)
    conv = cb_ref[...] + cw_ref[0:1, :] * a_s[6:6 + tm, :]
    conv = conv + cw_ref[1:2, :] * a_s[7:7 + tm, :]
    conv = conv + cw_ref[2:3, :] * a_s[8:8 + tm, :]
    a_s[0:8, :] = a_s[tm:tm + 8, :]
    gated = _gelu(conv) * jnp.dot(x1b, wb_ref[...], preferred_element_type=jnp.float32)
    ff = jnp.dot(gated.astype(jnp.bfloat16), wd_ref[...], preferred_element_type=jnp.float32)
    x2 = _layernorm_rows(alpha * x1 + ff, ln2g_ref[...], ln2b_ref[...])

    gate = jax.nn.sigmoid(jnp.dot(x2.astype(jnp.bfloat16), wg_ref[...], preferred_element_type=jnp.float32)
                          + bg_ref[...])
    ple = gate * jnp.dot(p_ref[...].astype(jnp.bfloat16), wp_ref[...], preferred_element_type=jnp.float32)
    o_ref[...] = _layernorm_rows(alpha * x2 + ple, ln3g_ref[...], ln3b_ref[...])


def _const_spec(shape):
    nd = len(shape)
    return pl.BlockSpec(shape, lambda *_: (0,) * nd, pipeline_mode=pl.Buffered(1))


def _layer(x, p, positions, w_in, ln_z_g, ln_z_b, w_s, b_s, w_o, ln1_g, ln1_b, w_ff_a, w_ff_b, conv_w, conv_b,
           w_ff_down, ln2_g, ln2_b, w_ple_gate, b_ple_gate, w_ple_in, ln3_g, ln3_b, *, alpha, tm1, tm3):
    B, S, D = x.shape
    N = B * S
    F = w_ff_a.shape[1]
    bf = jnp.bfloat16
    xf = x.reshape(N, D)

    inv = np.float32(ROPE_THETA ** (-np.arange(0, ROPE_DIM, 2, dtype=np.float64) / ROPE_DIM)).reshape(8, 1)
    wqkv = w_in[:, :3 * D_ATTN].astype(bf)
    wuzT = w_in[:, 3 * D_ATTN:].T.astype(bf)
    pos3 = positions.reshape(N // tm1, 1, tm1)
    tps1 = S // tm1
    n_l = S // N_RES
    row = lambda v: v.reshape(1, -1)
    q, k, v, gm = pl.pallas_call(
        functools.partial(_proj_kernel, tm=tm1),
        grid=(N // tm1,),
        in_specs=[
            pl.BlockSpec((tm1, D), lambda i: (i, 0)),
            pl.BlockSpec((None, 1, tm1), lambda i: (i, 0, 0)),
            _const_spec((8, 1)),
            _const_spec((D, 3 * D_ATTN)),
            _const_spec((2 * D_GMLP, D)),
            _const_spec((D_GMLP, 1)),
            _const_spec((D_GMLP, 1)),
            _const_spec((N_GROUPS, CHUNK, CHUNK)),
            _const_spec((N_GROUPS, CHUNK)),
        ],
        out_specs=[pl.BlockSpec((N_PAIRS, None, N_RES, tm1 // N_RES, LANES),
                                lambda i: (0, i // tps1, 0, i % tps1, 0))] * 3
        + [pl.BlockSpec((tm1, D_GMLP), lambda i: (i, 0))],
        out_shape=[jax.ShapeDtypeStruct((N_PAIRS, B, N_RES, n_l, LANES), jnp.float32)] * 3
        + [jax.ShapeDtypeStruct((N, D_GMLP), bf)],
        scratch_shapes=[pltpu.VMEM((3 * N_PAIRS, tm1, LANES), jnp.float32)] * 2,
        compiler_params=pltpu.CompilerParams(dimension_semantics=("arbitrary",), vmem_limit_bytes=VMEM_LIMIT),
        name="proj",
    )(xf, pos3, jnp.asarray(inv), wqkv, wuzT, ln_z_g.reshape(-1, 1), ln_z_b.reshape(-1, 1),
      jnp.swapaxes(w_s, 1, 2), b_s)

    masks = jnp.asarray(_branch_masks())
    seq_spec = lambda n: pl.BlockSpec((N_PAIRS, None, N_RES, n_l, LANES), lambda b: (0, b, 0, 0, 0),
                                      pipeline_mode=pl.Buffered(n))
    state = pltpu.VMEM((N_PAIRS, N_RES, WINDOW_BLOCK, LANES), jnp.float32)
    attn = pl.pallas_call(
        _attn_kernel,
        grid=(B,),
        in_specs=[seq_spec(1), seq_spec(1), seq_spec(1), _const_spec(masks.shape)],
        out_specs=pl.BlockSpec((N_PAIRS, None, N_RES, n_l, LANES), lambda b: (0, b, 0, 0, 0)),
        out_shape=jax.ShapeDtypeStruct((N_PAIRS, B, N_RES, n_l, LANES), bf),
        scratch_shapes=[state, state, state],
        compiler_params=pltpu.CompilerParams(dimension_semantics=("arbitrary",), vmem_limit_bytes=VMEM_LIMIT),
        name="attn",
    )(q, k, v, masks)

    Dp = p.shape[-1]
    tps3 = S // tm3
    out = pl.pallas_call(
        functools.partial(_tail_kernel, tm=tm3, tiles_per_seq=tps3, alpha=alpha),
        grid=(N // tm3,),
        in_specs=[
            pl.BlockSpec((N_PAIRS, None, N_RES, tm3 // N_RES, LANES), lambda i: (0, i // tps3, 0, i % tps3, 0)),
            pl.BlockSpec((tm3, D_GMLP), lambda i: (i, 0)),
            pl.BlockSpec((tm3, D), lambda i: (i, 0)),
            pl.BlockSpec((tm3, Dp), lambda i: (i, 0)),
            _const_spec((D, D)), _const_spec((1, D)), _const_spec((1, D)),
            _const_spec((D, F)), _const_spec((D, F)), _const_spec((3, F)), _const_spec((1, F)),
            _const_spec((F, D)), _const_spec((1, D)), _const_spec((1, D)),
            _const_spec((D, D)), _const_spec((1, D)), _const_spec((Dp, D)), _const_spec((1, D)), _const_spec((1, D)),
        ],
        out_specs=pl.BlockSpec((tm3, D), lambda i: (i, 0)),
        out_shape=jax.ShapeDtypeStruct((N, D), jnp.float32),
        scratch_shapes=[pltpu.VMEM((tm3 + 8, F), jnp.float32), pltpu.VMEM((N_PAIRS, tm3, LANES), jnp.float32)],
        compiler_params=pltpu.CompilerParams(dimension_semantics=("arbitrary",), vmem_limit_bytes=VMEM_LIMIT),
        name="tail",
    )(attn, gm, xf, p.reshape(N, Dp), w_o.astype(bf), row(ln1_g), row(ln1_b),
      w_ff_a.astype(bf), w_ff_b.astype(bf), conv_w, row(conv_b), w_ff_down.astype(bf), row(ln2_g), row(ln2_b),
      w_ple_gate.astype(bf), row(b_ple_gate), w_ple_in.astype(bf), row(ln3_g), row(ln3_b))
    return out.reshape(B, S, D)


def kernel(x, p, positions, w_in, ln_z_g, ln_z_b, w_s, b_s, w_o, ln1_g, ln1_b, w_ff_a, w_ff_b, conv_w, conv_b,
           w_ff_down, ln2_g, ln2_b, w_ple_gate, b_ple_gate, w_ple_in, ln3_g, ln3_b):
    depth = w_in.shape[0]
    alpha = (2.0 * depth) ** 0.25
    for i in range(depth):
        x = _layer(x, p[i], positions, w_in[i], ln_z_g[i], ln_z_b[i], w_s[i], b_s[i], w_o[i], ln1_g[i], ln1_b[i],
                   w_ff_a[i], w_ff_b[i], conv_w[i], conv_b[i], w_ff_down[i], ln2_g[i], ln2_b[i],
                   w_ple_gate[i], b_ple_gate[i], w_ple_in[i], ln3_g[i], ln3_b[i],
                   alpha=alpha, tm1=512, tm3=256)
    return x
```

```python
import functools
import math

import numpy as np
import jax
import jax.numpy as jnp
from jax import lax
from jax.experimental import pallas as pl
from jax.experimental.pallas import tpu as pltpu

HEAD_DIM = 64
N_HEADS = 8
D_ATTN = 512
D_GMLP = 512
N_GROUPS = 8
CHUNK = 128
ROPE_THETA = 500000.0
ROPE_DIM = 16
LN_EPS = 1e-5
NEG_INF = -1e30
WINDOW_BLOCK = 128
LANES = 128
N_PAIRS = D_ATTN // LANES
N_RES = 16
VMEM_LIMIT = 56 * 1024 * 1024

_INV_SQRT2 = 0.7071067811865476


def _gelu(t):
    return 0.5 * t * (1.0 + lax.erf(t * _INV_SQRT2))


def _layernorm_rows(t, g, b):
    mu = jnp.mean(t, axis=-1, keepdims=True)
    d = t - mu
    var = jnp.mean(d * d, axis=-1, keepdims=True)
    return d * lax.rsqrt(var + LN_EPS) * g + b


def _proj_kernel(x_ref, pos_ref, inv_ref, wqkv_ref, wuzT_ref, lnzg_ref, lnzb_ref, wsT_ref, bs_ref,
                 q_ref, k_ref, v_ref, gm_ref, nat_s, mid_s, *, tm):
    xb = x_ref[...].astype(jnp.bfloat16)
    h = jnp.dot(xb, wqkv_ref[...], preferred_element_type=jnp.float32)

    pos = pos_ref[...].astype(jnp.float32)
    ang = pos * inv_ref[...]
    cos_t = jnp.cos(ang)
    sin_t = jnp.sin(ang)
    ones48 = jnp.ones((HEAD_DIM - ROPE_DIM, tm), jnp.float32)
    zeros48 = jnp.zeros((HEAD_DIM - ROPE_DIM, tm), jnp.float32)
    c_t = jnp.concatenate([cos_t, cos_t, ones48, cos_t, cos_t, ones48], axis=0)
    s_t = jnp.concatenate([-sin_t, sin_t, zeros48, -sin_t, sin_t, zeros48], axis=0)
    c_tab = c_t.T
    s_tab = s_t.T
    lane = lax.broadcasted_iota(jnp.int32, (1, LANES), 1)
    first_half = (lane % HEAD_DIM) < (ROPE_DIM // 2)

    def rope(t):
        up = pltpu.roll(t, LANES - ROPE_DIM // 2, 1)
        dn = pltpu.roll(t, ROPE_DIM // 2, 1)
        return t * c_tab + jnp.where(first_half, up, dn) * s_tab

    scale = math.log2(math.e) / math.sqrt(HEAD_DIM)
    for hp in range(N_PAIRS):
        lo = hp * LANES
        nat_s[hp] = rope(h[:, lo:lo + LANES]) * scale
        nat_s[N_PAIRS + hp] = rope(h[:, D_ATTN + lo:D_ATTN + lo + LANES])
        nat_s[2 * N_PAIRS + hp] = h[:, 2 * D_ATTN + lo:2 * D_ATTN + lo + LANES]

    quarter = tm // 4
    n_l = tm // N_RES
    for idx in range(3 * N_PAIRS):
        for r in range(4):
            mid_s[idx, r * quarter:(r + 1) * quarter, :] = nat_s[idx, pl.ds(r, quarter, stride=4), :]
    for which, o_ref in enumerate((q_ref, k_ref, v_ref)):
        for hp in range(N_PAIRS):
            idx = which * N_PAIRS + hp
            for r_lo in range(4):
                for r_hi in range(4):
                    o_ref[hp, 4 * r_hi + r_lo] = mid_s[idx, pl.ds(r_lo * quarter + r_hi, n_l, stride=4), :]

    h_t = lax.dot_general(wuzT_ref[...], xb, (((1,), (1,)), ((), ())),
                          preferred_element_type=jnp.float32)
    u_t = _gelu(h_t[:D_GMLP])
    z_t = _gelu(h_t[D_GMLP:])
    mu = jnp.mean(z_t, axis=0, keepdims=True)
    d = z_t - mu
    var = jnp.mean(d * d, axis=0, keepdims=True)
    zn_t = (d * lax.rsqrt(var + LN_EPS) * lnzg_ref[...] + lnzb_ref[...]).astype(jnp.bfloat16)

    n_chunks = tm // CHUNK
    row = lax.broadcasted_iota(jnp.int32, (CHUNK, CHUNK), 0)
    col = lax.broadcasted_iota(jnp.int32, (CHUNK, CHUNK), 1)
    keep = row <= col
    mixed = []
    for g in range(N_GROUPS):
        r0 = g * HEAD_DIM
        lhs = jnp.concatenate([zn_t[r0:r0 + HEAD_DIM, c * CHUNK:(c + 1) * CHUNK] for c in range(n_chunks)],
                              axis=0)
        w = jnp.where(keep, wsT_ref[g], 0.0).astype(jnp.bfloat16)
        res = jnp.dot(lhs, w, preferred_element_type=jnp.float32)
        res = res + bs_ref[g:g + 1, :]
        mixed.append(jnp.concatenate([res[c * HEAD_DIM:(c + 1) * HEAD_DIM] for c in range(n_chunks)], axis=1))
    mixed_t = jnp.concatenate(mixed, axis=0)
    gm_ref[...] = (u_t * mixed_t).T.astype(jnp.bfloat16)


def _branch_masks():
    w = WINDOW_BLOCK
    out = np.zeros((6, w, 2 * w), np.float32)
    rq, i8 = np.divmod(np.arange(w), 8)
    rk, j16 = np.divmod(np.arange(2 * w), 16)
    base = 16 * (i8[:, None] - j16[None, :]) + (rq[:, None] - rk[None, :])
    aq, i32 = np.divmod(np.arange(w), 32)
    ak, j64 = np.divmod(np.arange(2 * w), 64)
    base4 = 4 * (i32[:, None] - j64[None, :]) + (aq[:, None] - ak[None, :])
    base16 = np.arange(w)[:, None] - np.arange(2 * w)[None, :]
    for c, b in enumerate((base, base4, base16)):
        dist = b + w
        out[2 * c] = (dist >= 0) & (dist <= w)
        dist0 = b
        out[2 * c + 1] = (dist0 >= 0) & (dist0 <= w)
    return out


def _attn_kernel(q_s, k_s, v_s, mask_ref, o_ref, acc_s, m_s, l_s):
    n_l = q_s.shape[2]
    n_spans = n_l // WINDOW_BLOCK

    lane = lax.broadcasted_iota(jnp.int32, (1, LANES), 1)
    head0 = lane < HEAD_DIM

    def gather(ref, hp, chunks):
        return jnp.concatenate([ref[hp, r, pl.ds(st, n), :] for (r, st, n) in chunks], axis=0)

    def for_pairs(q_chunks, kv_chunks, st_chunks, mask, first, last):
        w = WINDOW_BLOCK
        pairs = range(N_PAIRS)
        valid = mask > 0.5

        s2s = []
        for hp in pairs:
            qv = gather(q_s, hp, q_chunks)
            kb = gather(k_s, hp, kv_chunks).astype(jnp.bfloat16)
            q2 = jnp.concatenate([jnp.where(head0, qv, 0.0), jnp.where(head0, 0.0, qv)],
                                 axis=0).astype(jnp.bfloat16)
            s2s.append(lax.dot_general(q2, kb, (((1,), (1,)), ((), ())),
                                       preferred_element_type=jnp.float32))

        p2s, stats = [], []
        for hp in pairs:
            m_old = None if first else gather(m_s, hp, st_chunks)
            ps, ms, ls = [], [], []
            for hh in range(2):
                s = jnp.where(valid, s2s[hp][hh * w:(hh + 1) * w], NEG_INF)
                t = jnp.maximum(s[:, :LANES], s[:, LANES:])
                if not first:
                    own = head0 if hh == 0 else jnp.logical_not(head0)
                    t = jnp.maximum(t, jnp.where(own, m_old, NEG_INF))
                m_new = jnp.max(t, axis=1, keepdims=True)
                p = jnp.exp2(s - m_new)
                ps.append(p)
                ms.append(m_new)
                ls.append(jnp.sum(p, axis=1, keepdims=True))
            p2s.append(jnp.concatenate(ps, axis=0).astype(jnp.bfloat16))
            stats.append((m_old, jnp.where(head0, ms[0], ms[1]), jnp.where(head0, ls[0], ls[1])))

        o2s = []
        for hp in pairs:
            vb = gather(v_s, hp, kv_chunks).astype(jnp.bfloat16)
            o2s.append(jnp.dot(p2s[hp], vb, preferred_element_type=jnp.float32))

        for hp in pairs:
            m_old, m_b, l_b = stats[hp]
            o_pair = jnp.where(head0, o2s[hp][:w], o2s[hp][w:])
            if first:
                acc, l_tot = o_pair, l_b
            else:
                alpha = jnp.exp2(m_old - m_b)
                acc = gather(acc_s, hp, st_chunks) * alpha + o_pair
                l_tot = gather(l_s, hp, st_chunks) * alpha + l_b
            if last:
                (r, st, n), = q_chunks
                o_ref[hp, r, pl.ds(st, n), :] = (acc / l_tot).astype(o_ref.dtype)
            else:
                off = 0
                for (r, st, n) in st_chunks:
                    acc_s[hp, r, pl.ds(st, n), :] = acc[off:off + n]
                    m_s[hp, r, pl.ds(st, n), :] = m_b[off:off + n]
                    l_s[hp, r, pl.ds(st, n), :] = l_tot[off:off + n]
                    off += n

    for span in range(n_spans):
        def d1_body(n, carry, span=span):
            nb = span * 16 + n
            q0 = pl.multiple_of(nb * 8, 8)
            k0 = pl.multiple_of(jnp.maximum(nb - 1, 0) * 8, 8)
            s0 = pl.multiple_of(n * 8, 8)
            mask = mask_ref[jnp.where(nb == 0, 1, 0)]
            for_pairs([(r, q0, 8) for r in range(N_RES)], [(r, k0, 16) for r in range(N_RES)],
                      [(r, s0, 8) for r in range(N_RES)], mask, True, False)
            return carry
        lax.fori_loop(0, 16, d1_body, 0)

        def d4_body(it, carry, span=span):
            r4 = it // 4
            n = span * 4 + it % 4
            q0 = pl.multiple_of(n * 32, 32)
            k0 = pl.multiple_of(jnp.maximum(n - 1, 0) * 32, 32)
            s0 = pl.multiple_of((it % 4) * 32, 32)
            mask = mask_ref[2 + jnp.where(n == 0, 1, 0)]
            for_pairs([(r4 + 4 * a, q0, 32) for a in range(4)], [(r4 + 4 * a, k0, 64) for a in range(4)],
                      [(r4 + 4 * a, s0, 32) for a in range(4)], mask, False, False)
            return carry
        lax.fori_loop(0, 16, d4_body, 0)

        def d16_body(r16, carry, span=span):
            q0 = span * WINDOW_BLOCK
            k0 = max(span - 1, 0) * WINDOW_BLOCK
            mask = mask_ref[4 + (1 if span == 0 else 0)]
            for_pairs([(r16, q0, WINDOW_BLOCK)], [(r16, k0, 2 * WINDOW_BLOCK)],
                      [(r16, 0, WINDOW_BLOCK)], mask, False, True)
            return carry
        lax.fori_loop(0, N_RES, d16_body, 0)


def _tail_kernel(attn_ref, gm_ref, x_ref, p_ref, wo_ref, ln1g_ref, ln1b_ref, wa_ref, wb_ref, cw_ref, cb_ref,
                 wd_ref, ln2g_ref, ln2b_ref, wg_ref, bg_ref, wp_ref, ln3g_ref, ln3b_ref, o_ref, a_s, un_s,
                 *, tm, tiles_per_seq, alpha):
    i = pl.program_id(0)

    @pl.when(i % tiles_per_seq == 0)
    def _():
        a_s[0:8, :] = jnp.zeros((8, a_s.shape[1]), jnp.float32)

    n_l = tm // N_RES
    for hp in range(N_PAIRS):
        for r in range(N_RES):
            un_s[hp, pl.ds(r, n_l, stride=N_RES), :] = attn_ref[hp, r].astype(jnp.float32)
    mixin = jnp.concatenate([un_s[hp].astype(jnp.bfloat16) for hp in range(N_PAIRS)] + [gm_ref[...]],
                            axis=1)
    mix = jnp.dot(mixin, wo_ref[...], preferred_element_type=jnp.float32)
    x1 = _layernorm_rows(alpha * x_ref[...] + mix, ln1g_ref[...], ln1b_ref[...])
    x1b = x1.astype(jnp.bfloat16)

    a_s[8:8 + tm, :] = jnp.dot(x1b, wa_ref[...], preferred_element_type=jnp.float32)
    conv = cb_ref[...] + cw_ref[0:1, :] * a_s[6:6 + tm, :]
    conv = conv + cw_ref[1:2, :] * a_s[7:7 + tm, :]
    conv = conv + cw_ref[2:3, :] * a_s[8:8 + tm, :]
    a_s[0:8, :] = a_s[tm:tm + 8, :]
    gated = _gelu(conv) * jnp.dot(x1b, wb_ref[...], preferred_element_type=jnp.float32)
    ff = jnp.dot(gated.astype(jnp.bfloat16), wd_ref[...], preferred_element_type=jnp.float32)
    x2 = _layernorm_rows(alpha * x1 + ff, ln2g_ref[...], ln2b_ref[...])

    gate = jax.nn.sigmoid(jnp.dot(x2.astype(jnp.bfloat16), wg_ref[...], preferred_element_type=jnp.float32)
                          + bg_ref[...])
    ple = gate * jnp.dot(p_ref[...].astype(jnp.bfloat16), wp_ref[...], preferred_element_type=jnp.float32)
    o_ref[...] = _layernorm_rows(alpha * x2 + ple, ln3g_ref[...], ln3b_ref[...])


def _const_spec(shape):
    nd = len(shape)
    return pl.BlockSpec(shape, lambda *_: (0,) * nd, pipeline_mode=pl.Buffered(1))


def _layer(x, p, positions, w_in, ln_z_g, ln_z_b, w_s, b_s, w_o, ln1_g, ln1_b, w_ff_a, w_ff_b, conv_w, conv_b,
           w_ff_down, ln2_g, ln2_b, w_ple_gate, b_ple_gate, w_ple_in, ln3_g, ln3_b, *, alpha, tm1, tm3):
    B, S, D = x.shape
    N = B * S
    F = w_ff_a.shape[1]
    bf = jnp.bfloat16
    xf = x.reshape(N, D)

    inv = np.float32(ROPE_THETA ** (-np.arange(0, ROPE_DIM, 2, dtype=np.float64) / ROPE_DIM)).reshape(8, 1)
    wqkv = w_in[:, :3 * D_ATTN].astype(bf)
    wuzT = w_in[:, 3 * D_ATTN:].T.astype(bf)
    pos3 = positions.reshape(N // tm1, 1, tm1)
    tps1 = S // tm1
    n_l = S // N_RES
    row = lambda v: v.reshape(1, -1)
    q, k, v, gm = pl.pallas_call(
        functools.partial(_proj_kernel, tm=tm1),
        grid=(N // tm1,),
        in_specs=[
            pl.BlockSpec((tm1, D), lambda i: (i, 0)),
            pl.BlockSpec((None, 1, tm1), lambda i: (i, 0, 0)),
            _const_spec((8, 1)),
            _const_spec((D, 3 * D_ATTN)),
            _const_spec((2 * D_GMLP, D)),
            _const_spec((D_GMLP, 1)),
            _const_spec((D_GMLP, 1)),
            _const_spec((N_GROUPS, CHUNK, CHUNK)),
            _const_spec((N_GROUPS, CHUNK)),
        ],
        out_specs=[pl.BlockSpec((N_PAIRS, None, N_RES, tm1 // N_RES, LANES),
                                lambda i: (0, i // tps1, 0, i % tps1, 0))] * 3
        + [pl.BlockSpec((tm1, D_GMLP), lambda i: (i, 0))],
        out_shape=[jax.ShapeDtypeStruct((N_PAIRS, B, N_RES, n_l, LANES), jnp.float32)] * 3
        + [jax.ShapeDtypeStruct((N, D_GMLP), bf)],
        scratch_shapes=[pltpu.VMEM((3 * N_PAIRS, tm1, LANES), jnp.float32)] * 2,
        compiler_params=pltpu.CompilerParams(dimension_semantics=("arbitrary",), vmem_limit_bytes=VMEM_LIMIT),
        name="proj",
    )(xf, pos3, jnp.asarray(inv), wqkv, wuzT, ln_z_g.reshape(-1, 1), ln_z_b.reshape(-1, 1),
      jnp.swapaxes(w_s, 1, 2), b_s)

    masks = jnp.asarray(_branch_masks())
    seq_spec = lambda n: pl.BlockSpec((N_PAIRS, None, N_RES, n_l, LANES), lambda b: (0, b, 0, 0, 0),
                                      pipeline_mode=pl.Buffered(n))
    state = pltpu.VMEM((N_PAIRS, N_RES, WINDOW_BLOCK, LANES), jnp.float32)
    attn = pl.pallas_call(
        _attn_kernel,
        grid=(B,),
        in_specs=[seq_spec(1), seq_spec(1), seq_spec(1), _const_spec(masks.shape)],
        out_specs=pl.BlockSpec((N_PAIRS, None, N_RES, n_l, LANES), lambda b: (0, b, 0, 0, 0)),
        out_shape=jax.ShapeDtypeStruct((N_PAIRS, B, N_RES, n_l, LANES), bf),
        scratch_shapes=[state, state, state],
        compiler_params=pltpu.CompilerParams(dimension_semantics=("arbitrary",), vmem_limit_bytes=VMEM_LIMIT),
        name="attn",
    )(q, k, v, masks)

    Dp = p.shape[-1]
    tps3 = S // tm3
    out = pl.pallas_call(
        functools.partial(_tail_kernel, tm=tm3, tiles_per_seq=tps3, alpha=alpha),
        grid=(N // tm3,),
        in_specs=[
            pl.BlockSpec((N_PAIRS, None, N_RES, tm3 // N_RES, LANES), lambda i: (0, i // tps3, 0, i % tps3, 0)),
            pl.BlockSpec((tm3, D_GMLP), lambda i: (i, 0)),
            pl.BlockSpec((tm3, D), lambda i: (i, 0)),
            pl.BlockSpec((tm3, Dp), lambda i: (i, 0)),
            _const_spec((D, D)), _const_spec((1, D)), _const_spec((1, D)),
            _const_spec((D, F)), _const_spec((D, F)), _const_spec((3, F)), _const_spec((1, F)),
            _const_spec((F, D)), _const_spec((1, D)), _const_spec((1, D)),
            _const_spec((D, D)), _const_spec((1, D)), _const_spec((Dp, D)), _const_spec((1, D)), _const_spec((1, D)),
        ],
        out_specs=pl.BlockSpec((tm3, D), lambda i: (i, 0)),
        out_shape=jax.ShapeDtypeStruct((N, D), jnp.float32),
        scratch_shapes=[pltpu.VMEM((tm3 + 8, F), jnp.float32), pltpu.VMEM((N_PAIRS, tm3, LANES), jnp.float32)],
        compiler_params=pltpu.CompilerParams(dimension_semantics=("arbitrary",), vmem_limit_bytes=VMEM_LIMIT),
        name="tail",
    )(attn, gm, xf, p.reshape(N, Dp), w_o.astype(bf), row(ln1_g), row(ln1_b),
      w_ff_a.astype(bf), w_ff_b.astype(bf), conv_w, row(conv_b), w_ff_down.astype(bf), row(ln2_g), row(ln2_b),
      w_ple_gate.astype(bf), row(b_ple_gate), w_ple_in.astype(bf), row(ln3_g), row(ln3_b))
    return out.reshape(B, S, D)


def kernel(x, p, positions, w_in, ln_z_g, ln_z_b, w_s, b_s, w_o, ln1_g, ln1_b, w_ff_a, w_ff_b, conv_w, conv_b,
           w_ff_down, ln2_g, ln2_b, w_ple_gate, b_ple_gate, w_ple_in, ln3_g, ln3_b):
    depth = w_in.shape[0]
    alpha = (2.0 * depth) ** 0.25
    for i in range(depth):
        x = _layer(x, p[i], positions, w_in[i], ln_z_g[i], ln_z_b[i], w_s[i], b_s[i], w_o[i], ln1_g[i], ln1_b[i],
                   w_ff_a[i], w_ff_b[i], conv_w[i], conv_b[i], w_ff_down[i], ln2_g[i], ln2_b[i],
                   w_ple_gate[i], b_ple_gate[i], w_ple_in[i], ln3_g[i], ln3_b[i],
                   alpha=alpha, tm1=512, tm3=256)
    return x
```

```python
import functools
import math

import numpy as np
import jax
import jax.numpy as jnp
from jax import lax
from jax.experimental import pallas as pl
from jax.experimental.pallas import tpu as pltpu

HEAD_DIM = 64
N_HEADS = 8
D_ATTN = 512
D_GMLP = 512
N_GROUPS = 8
CHUNK = 128
ROPE_THETA = 500000.0
ROPE_DIM = 16
LN_EPS = 1e-5
NEG_INF = -1e30
WINDOW_BLOCK = 128
LANES = 128
N_PAIRS = D_ATTN // LANES
N_RES = 16
BLOCKS_PER_STEP = 2
LOOKAHEAD = 4
VMEM_LIMIT = 56 * 1024 * 1024

_INV_SQRT2 = 0.7071067811865476


def _gelu(t):
    return 0.5 * t * (1.0 + lax.erf(t * _INV_SQRT2))


def _layernorm_rows(t, g, b):
    mu = jnp.mean(t, axis=-1, keepdims=True)
    d = t - mu
    var = jnp.mean(d * d, axis=-1, keepdims=True)
    return d * lax.rsqrt(var + LN_EPS) * g + b


def _proj_kernel(x_ref, pos_ref, inv_ref, wqkv_ref, wuzT_ref, lnzg_ref, lnzb_ref, wsT_ref, bs_ref,
                 q_ref, k_ref, v_ref, gm_ref, nat_s, mid_s, *, tm, ts):
    subs = [(s * ts, ts) for s in range(tm // ts)]
    f32, bf = jnp.float32, jnp.bfloat16
    xb = [x_ref[r0:r0 + n, :].astype(bf) for (r0, n) in subs]
    h_t = [lax.dot_general(wuzT_ref[...], xb[s], (((1,), (1,)), ((), ())), preferred_element_type=f32)
           for s in range(len(subs))]
    h = [jnp.dot(xb[s], wqkv_ref[...], preferred_element_type=f32) for s in range(len(subs))]

    row = lax.broadcasted_iota(jnp.int32, (CHUNK, CHUNK), 0)
    col = lax.broadcasted_iota(jnp.int32, (CHUNK, CHUNK), 1)
    keep = row <= col
    w_sp = [jnp.where(keep, wsT_ref[g], 0.0).astype(bf) for g in range(N_GROUPS)]
    n_chunks = ts // CHUNK
    for s, (t0, n) in enumerate(subs):
        u_t = _gelu(h_t[s][:D_GMLP])
        z_t = _gelu(h_t[s][D_GMLP:])
        mu = jnp.mean(z_t, axis=0, keepdims=True)
        d = z_t - mu
        var = jnp.mean(d * d, axis=0, keepdims=True)
        zn_t = (d * lax.rsqrt(var + LN_EPS) * lnzg_ref[...] + lnzb_ref[...]).astype(bf)
        mixed = []
        for g in range(N_GROUPS):
            r0 = g * HEAD_DIM
            lhs = jnp.concatenate([zn_t[r0:r0 + HEAD_DIM, c * CHUNK:(c + 1) * CHUNK] for c in range(n_chunks)],
                                  axis=0)
            res = jnp.dot(lhs, w_sp[g], preferred_element_type=f32) + bs_ref[g:g + 1, :]
            mixed.append(jnp.concatenate([res[c * HEAD_DIM:(c + 1) * HEAD_DIM] for c in range(n_chunks)],
                                         axis=1))
        mixed_t = jnp.concatenate(mixed, axis=0)
        gm_ref[t0:t0 + n, :] = (u_t * mixed_t).T.astype(bf)

    pos = pos_ref[...].astype(f32)
    ang = pos * inv_ref[...]
    cos_t = jnp.cos(ang)
    sin_t = jnp.sin(ang)
    ones48 = jnp.ones((HEAD_DIM - ROPE_DIM, tm), f32)
    zeros48 = jnp.zeros((HEAD_DIM - ROPE_DIM, tm), f32)
    c_tab = jnp.concatenate([cos_t, cos_t, ones48, cos_t, cos_t, ones48], axis=0).T
    s_tab = jnp.concatenate([-sin_t, sin_t, zeros48, -sin_t, sin_t, zeros48], axis=0).T
    lane = lax.broadcasted_iota(jnp.int32, (1, LANES), 1)
    first_half = (lane % HEAD_DIM) < (ROPE_DIM // 2)

    def rope(t, t0, n):
        up = pltpu.roll(t, LANES - ROPE_DIM // 2, 1)
        dn = pltpu.roll(t, ROPE_DIM // 2, 1)
        return t * c_tab[t0:t0 + n] + jnp.where(first_half, up, dn) * s_tab[t0:t0 + n]

    scale = math.log2(math.e) / math.sqrt(HEAD_DIM)
    for s, (t0, n) in enumerate(subs):
        for hp in range(N_PAIRS):
            lo = hp * LANES
            nat_s[hp, t0:t0 + n, :] = rope(h[s][:, lo:lo + LANES], t0, n) * scale
            nat_s[N_PAIRS + hp, t0:t0 + n, :] = rope(h[s][:, D_ATTN + lo:D_ATTN + lo + LANES], t0, n)
            nat_s[2 * N_PAIRS + hp, t0:t0 + n, :] = h[s][:, 2 * D_ATTN + lo:2 * D_ATTN + lo + LANES]

    quarter = tm // 4
    n_l = tm // N_RES
    for idx in range(3 * N_PAIRS):
        for r in range(4):
            mid_s[idx, r * quarter:(r + 1) * quarter, :] = nat_s[idx, pl.ds(r, quarter, stride=4), :]
    for which, o_ref in enumerate((q_ref, k_ref, v_ref)):
        for hp in range(N_PAIRS):
            idx = which * N_PAIRS + hp
            for r_lo in range(4):
                for r_hi in range(4):
                    o_ref[hp, 4 * r_hi + r_lo] = mid_s[idx, pl.ds(r_lo * quarter + r_hi, n_l, stride=4), :]


def _branch_masks():
    w = WINDOW_BLOCK
    out = np.zeros((6, w, 2 * w), np.float32)
    rq, i8 = np.divmod(np.arange(w), 8)
    rk, j16 = np.divmod(np.arange(2 * w), 16)
    base = 16 * (i8[:, None] - j16[None, :]) + (rq[:, None] - rk[None, :])
    aq, i32 = np.divmod(np.arange(w), 32)
    ak, j64 = np.divmod(np.arange(2 * w), 64)
    base4 = 4 * (i32[:, None] - j64[None, :]) + (aq[:, None] - ak[None, :])
    base16 = np.arange(w)[:, None] - np.arange(2 * w)[None, :]
    for c, b in enumerate((base, base4, base16)):
        dist = b + w
        out[2 * c] = (dist >= 0) & (dist <= w)
        dist0 = b
        out[2 * c + 1] = (dist0 >= 0) & (dist0 <= w)
    return out


def _attn_kernel(q_s, k_s, v_s, mask_ref, o_ref, *state):
    acc_s, m_s, l_s = state[:N_PAIRS], state[N_PAIRS:2 * N_PAIRS], state[2 * N_PAIRS:]
    n_l = q_s.shape[2]
    n_spans = n_l // WINDOW_BLOCK

    lane = lax.broadcasted_iota(jnp.int32, (1, LANES), 1)
    head0 = lane < HEAD_DIM

    def gather(ref, chunks):
        return jnp.concatenate([ref[r, pl.ds(st, n), :] for (r, st, n) in chunks], axis=0)

    def for_blocks(blocks, first, last):
        w = WINDOW_BLOCK
        chains = [(blk, hp) for blk in blocks for hp in range(N_PAIRS)]
        valids = {id(blk): blk[3] > 0.5 for blk in blocks}

        def scores(c):
            (q_chunks, kv_chunks, _, _), hp = chains[c]
            qv = gather(q_s.at[hp], q_chunks)
            kb = gather(k_s.at[hp], kv_chunks).astype(jnp.bfloat16)
            q2 = jnp.concatenate([jnp.where(head0, qv, 0.0), jnp.where(head0, 0.0, qv)],
                                 axis=0).astype(jnp.bfloat16)
            return lax.dot_general(q2, kb, (((1,), (1,)), ((), ())),
                                   preferred_element_type=jnp.float32)

        def finish(c, s2):
            (q_chunks, kv_chunks, st_chunks, _), hp = chains[c]
            valid = valids[id(chains[c][0])]
            m_old = None if first else gather(m_s[hp], st_chunks)
            ps, ms, ls = [], [], []
            for hh in range(2):
                s = jnp.where(valid, s2[hh * w:(hh + 1) * w], NEG_INF)
                t = jnp.maximum(s[:, :LANES], s[:, LANES:])
                if not first:
                    own = head0 if hh == 0 else jnp.logical_not(head0)
                    t = jnp.maximum(t, jnp.where(own, m_old, NEG_INF))
                m_new = jnp.max(t, axis=1, keepdims=True)
                p = jnp.exp2(s - m_new)
                ps.append(p)
                ms.append(m_new)
                ls.append(jnp.sum(p, axis=1, keepdims=True))
            p2 = jnp.concatenate(ps, axis=0).astype(jnp.bfloat16)
            m_b = jnp.where(head0, ms[0], ms[1])
            l_b = jnp.where(head0, ls[0], ls[1])
            vb = gather(v_s.at[hp], kv_chunks).astype(jnp.bfloat16)
            o2 = jnp.dot(p2, vb, preferred_element_type=jnp.float32)
            o_pair = jnp.where(head0, o2[:w], o2[w:])
            if first:
                acc, l_tot = o_pair, l_b
            else:
                alpha = jnp.exp2(m_old - m_b)
                acc = gather(acc_s[hp], st_chunks) * alpha + o_pair
                l_tot = gather(l_s[hp], st_chunks) * alpha + l_b
            if last:
                (r, st, n), = q_chunks
                o_ref[hp, r, pl.ds(st, n), :] = (acc / l_tot).astype(o_ref.dtype)
            else:
                off = 0
                for (r, st, n) in st_chunks:
                    acc_s[hp][r, pl.ds(st, n), :] = acc[off:off + n]
                    m_s[hp][r, pl.ds(st, n), :] = m_b[off:off + n]
                    l_s[hp][r, pl.ds(st, n), :] = l_tot[off:off + n]
                    off += n

        pending = {c: scores(c) for c in range(min(LOOKAHEAD, len(chains)))}
        for c in range(len(chains)):
            s2 = pending.pop(c)
            if c + LOOKAHEAD < len(chains):
                pending[c + LOOKAHEAD] = scores(c + LOOKAHEAD)
            finish(c, s2)

    for span in range(n_spans):
        def d1_block(n, span=span):
            nb = span * 16 + n
            q0 = pl.multiple_of(nb * 8, 8)
            k0 = pl.multiple_of(jnp.maximum(nb - 1, 0) * 8, 8)
            s0 = pl.multiple_of(n * 8, 8)
            return ([(r, q0, 8) for r in range(N_RES)], [(r, k0, 16) for r in range(N_RES)],
                    [(r, s0, 8) for r in range(N_RES)], mask_ref[jnp.where(nb == 0, 1, 0)])

        def d1_body(it, carry):
            for_blocks([d1_block(BLOCKS_PER_STEP * it + j) for j in range(BLOCKS_PER_STEP)], True, False)
            return carry
        lax.fori_loop(0, 16 // BLOCKS_PER_STEP, d1_body, 0)

        def d4_block(idx, span=span):
            r4 = idx // 4
            n = span * 4 + idx % 4
            q0 = pl.multiple_of(n * 32, 32)
            k0 = pl.multiple_of(jnp.maximum(n - 1, 0) * 32, 32)
            s0 = pl.multiple_of((idx % 4) * 32, 32)
            return ([(r4 + 4 * a, q0, 32) for a in range(4)], [(r4 + 4 * a, k0, 64) for a in range(4)],
                    [(r4 + 4 * a, s0, 32) for a in range(4)], mask_ref[2 + jnp.where(n == 0, 1, 0)])

        def d4_body(it, carry):
            for_blocks([d4_block(BLOCKS_PER_STEP * it + j) for j in range(BLOCKS_PER_STEP)], False, False)
            return carry
        lax.fori_loop(0, 16 // BLOCKS_PER_STEP, d4_body, 0)

        def d16_block(r16, span=span):
            q0 = span * WINDOW_BLOCK
            k0 = max(span - 1, 0) * WINDOW_BLOCK
            return ([(r16, q0, WINDOW_BLOCK)], [(r16, k0, 2 * WINDOW_BLOCK)],
                    [(r16, 0, WINDOW_BLOCK)], mask_ref[4 + (1 if span == 0 else 0)])

        def d16_body(it, carry):
            for_blocks([d16_block(BLOCKS_PER_STEP * it + j) for j in range(BLOCKS_PER_STEP)], False, True)
            return carry
        lax.fori_loop(0, N_RES // BLOCKS_PER_STEP, d16_body, 0)


def _tail_kernel(attn_ref, gm_ref, x_ref, p_ref, wo_ref, ln1g_ref, ln1b_ref, wa_ref, wb_ref, cw_ref, cb_ref,
                 wd_ref, ln2g_ref, ln2b_ref, wg_ref, bg_ref, wp_ref, ln3g_ref, ln3b_ref, o_ref, a_s, un_s,
                 *, tm, ts, tiles_per_seq, alpha):
    i = pl.program_id(0)

    @pl.when(i % tiles_per_seq == 0)
    def _():
        a_s[0:8, :] = jnp.zeros((8, a_s.shape[1]), jnp.float32)

    n_l = tm // N_RES
    for hp in range(N_PAIRS):
        for r in range(N_RES):
            un_s[hp, pl.ds(r, n_l, stride=N_RES), :] = attn_ref[hp, r].astype(jnp.float32)

    subs = [(s * ts, ts) for s in range(tm // ts)]
    f32, bf = jnp.float32, jnp.bfloat16
    mm = lambda a, w_ref: jnp.dot(a, w_ref[...], preferred_element_type=f32)

    mix = []
    for (r0, n) in subs:
        mixin = jnp.concatenate([un_s[hp, r0:r0 + n, :].astype(bf) for hp in range(N_PAIRS)]
                                + [gm_ref[r0:r0 + n, :]], axis=1)
        mix.append(mm(mixin, wo_ref))
    pe = [mm(p_ref[r0:r0 + n, :].astype(bf), wp_ref) for (r0, n) in subs]
    x1 = [_layernorm_rows(alpha * x_ref[r0:r0 + n, :] + mix[s], ln1g_ref[...], ln1b_ref[...])
          for s, (r0, n) in enumerate(subs)]
    x1b = [t.astype(bf) for t in x1]
    hb = []
    for s, (r0, n) in enumerate(subs):
        a_s[8 + r0:8 + r0 + n, :] = mm(x1b[s], wa_ref)
        hb.append(mm(x1b[s], wb_ref))
    ff = []
    for s, (r0, n) in enumerate(subs):
        conv = cb_ref[...] + cw_ref[0:1, :] * a_s[6 + r0:6 + r0 + n, :]
        conv = conv + cw_ref[1:2, :] * a_s[7 + r0:7 + r0 + n, :]
        conv = conv + cw_ref[2:3, :] * a_s[8 + r0:8 + r0 + n, :]
        ff.append(mm((_gelu(conv) * hb[s]).astype(bf), wd_ref))
    a_s[0:8, :] = a_s[tm:tm + 8, :]
    x2 = [_layernorm_rows(alpha * x1[s] + ff[s], ln2g_ref[...], ln2b_ref[...]) for s in range(len(subs))]
    gate = [jax.nn.sigmoid(mm(x2[s].astype(bf), wg_ref) + bg_ref[...]) for s in range(len(subs))]
    for s, (r0, n) in enumerate(subs):
        o_ref[r0:r0 + n, :] = _layernorm_rows(alpha * x2[s] + gate[s] * pe[s], ln3g_ref[...], ln3b_ref[...])


def _const_spec(shape):
    nd = len(shape)
    return pl.BlockSpec(shape, lambda *_: (0,) * nd, pipeline_mode=pl.Buffered(1))


def _layer(x, p, positions, w_in, ln_z_g, ln_z_b, w_s, b_s, w_o, ln1_g, ln1_b, w_ff_a, w_ff_b, conv_w, conv_b,
           w_ff_down, ln2_g, ln2_b, w_ple_gate, b_ple_gate, w_ple_in, ln3_g, ln3_b, *, alpha, tm1, ts1, tm3, ts3):
    B, S, D = x.shape
    N = B * S
    F = w_ff_a.shape[1]
    bf = jnp.bfloat16
    xf = x.reshape(N, D)

    inv = np.float32(ROPE_THETA ** (-np.arange(0, ROPE_DIM, 2, dtype=np.float64) / ROPE_DIM)).reshape(8, 1)
    wqkv = w_in[:, :3 * D_ATTN].astype(bf)
    wuzT = w_in[:, 3 * D_ATTN:].T.astype(bf)
    pos3 = positions.reshape(N // tm1, 1, tm1)
    tps1 = S // tm1
    n_l = S // N_RES
    row = lambda v: v.reshape(1, -1)
    q, k, v, gm = pl.pallas_call(
        functools.partial(_proj_kernel, tm=tm1, ts=ts1),
        grid=(N // tm1,),
        in_specs=[
            pl.BlockSpec((tm1, D), lambda i: (i, 0)),
            pl.BlockSpec((None, 1, tm1), lambda i: (i, 0, 0)),
            _const_spec((8, 1)),
            _const_spec((D, 3 * D_ATTN)),
            _const_spec((2 * D_GMLP, D)),
            _const_spec((D_GMLP, 1)),
            _const_spec((D_GMLP, 1)),
            _const_spec((N_GROUPS, CHUNK, CHUNK)),
            _const_spec((N_GROUPS, CHUNK)),
        ],
        out_specs=[pl.BlockSpec((N_PAIRS, None, N_RES, tm1 // N_RES, LANES),
                                lambda i: (0, i // tps1, 0, i % tps1, 0))] * 3
        + [pl.BlockSpec((tm1, D_GMLP), lambda i: (i, 0))],
        out_shape=[jax.ShapeDtypeStruct((N_PAIRS, B, N_RES, n_l, LANES), jnp.float32)] * 3
        + [jax.ShapeDtypeStruct((N, D_GMLP), bf)],
        scratch_shapes=[pltpu.VMEM((3 * N_PAIRS, tm1, LANES), jnp.float32)] * 2,
        compiler_params=pltpu.CompilerParams(dimension_semantics=("arbitrary",), vmem_limit_bytes=VMEM_LIMIT),
        name="proj",
    )(xf, pos3, jnp.asarray(inv), wqkv, wuzT, ln_z_g.reshape(-1, 1), ln_z_b.reshape(-1, 1),
      jnp.swapaxes(w_s, 1, 2), b_s)

    masks = jnp.asarray(_branch_masks())
    seq_spec = lambda n: pl.BlockSpec((N_PAIRS, None, N_RES, n_l, LANES), lambda b: (0, b, 0, 0, 0),
                                      pipeline_mode=pl.Buffered(n))
    state = pltpu.VMEM((N_RES, WINDOW_BLOCK, LANES), jnp.float32)
    attn = pl.pallas_call(
        _attn_kernel,
        grid=(B,),
        in_specs=[seq_spec(1), seq_spec(1), seq_spec(1), _const_spec(masks.shape)],
        out_specs=pl.BlockSpec((N_PAIRS, None, N_RES, n_l, LANES), lambda b: (0, b, 0, 0, 0)),
        out_shape=jax.ShapeDtypeStruct((N_PAIRS, B, N_RES, n_l, LANES), bf),
        scratch_shapes=[state] * (3 * N_PAIRS),
        compiler_params=pltpu.CompilerParams(dimension_semantics=("arbitrary",), vmem_limit_bytes=VMEM_LIMIT),
        name="attn",
    )(q, k, v, masks)

    Dp = p.shape[-1]
    tps3 = S // tm3
    out = pl.pallas_call(
        functools.partial(_tail_kernel, tm=tm3, ts=ts3, tiles_per_seq=tps3, alpha=alpha),
        grid=(N // tm3,),
        in_specs=[
            pl.BlockSpec((N_PAIRS, None, N_RES, tm3 // N_RES, LANES), lambda i: (0, i // tps3, 0, i % tps3, 0)),
            pl.BlockSpec((tm3, D_GMLP), lambda i: (i, 0)),
            pl.BlockSpec((tm3, D), lambda i: (i, 0)),
            pl.BlockSpec((tm3, Dp), lambda i: (i, 0)),
            _const_spec((D, D)), _const_spec((1, D)), _const_spec((1, D)),
            _const_spec((D, F)), _const_spec((D, F)), _const_spec((3, F)), _const_spec((1, F)),
            _const_spec((F, D)), _const_spec((1, D)), _const_spec((1, D)),
            _const_spec((D, D)), _const_spec((1, D)), _const_spec((Dp, D)), _const_spec((1, D)), _const_spec((1, D)),
        ],
        out_specs=pl.BlockSpec((tm3, D), lambda i: (i, 0)),
        out_shape=jax.ShapeDtypeStruct((N, D), jnp.float32),
        scratch_shapes=[pltpu.VMEM((tm3 + 8, F), jnp.float32), pltpu.VMEM((N_PAIRS, tm3, LANES), jnp.float32)],
        compiler_params=pltpu.CompilerParams(dimension_semantics=("arbitrary",), vmem_limit_bytes=VMEM_LIMIT),
        name="tail",
    )(attn, gm, xf, p.reshape(N, Dp), w_o.astype(bf), row(ln1_g), row(ln1_b),
      w_ff_a.astype(bf), w_ff_b.astype(bf), conv_w, row(conv_b), w_ff_down.astype(bf), row(ln2_g), row(ln2_b),
      w_ple_gate.astype(bf), row(b_ple_gate), w_ple_in.astype(bf), row(ln3_g), row(ln3_b))
    return out.reshape(B, S, D)


def kernel(x, p, positions, w_in, ln_z_g, ln_z_b, w_s, b_s, w_o, ln1_g, ln1_b, w_ff_a, w_ff_b, conv_w, conv_b,
           w_ff_down, ln2_g, ln2_b, w_ple_gate, b_ple_gate, w_ple_in, ln3_g, ln3_b):
    depth = w_in.shape[0]
    alpha = (2.0 * depth) ** 0.25
    for i in range(depth):
        x = _layer(x, p[i], positions, w_in[i], ln_z_g[i], ln_z_b[i], w_s[i], b_s[i], w_o[i], ln1_g[i], ln1_b[i],
                   w_ff_a[i], w_ff_b[i], conv_w[i], conv_b[i], w_ff_down[i], ln2_g[i], ln2_b[i],
                   w_ple_gate[i], b_ple_gate[i], w_ple_in[i], ln3_g[i], ln3_b[i],
                   alpha=alpha, tm1=512, ts1=256, tm3=512, ts3=256)
    return x
```

```python
import functools
import math

import numpy as np
import jax
import jax.numpy as jnp
from jax import lax
from jax.experimental import pallas as pl
from jax.experimental.pallas import tpu as pltpu

HEAD_DIM = 64
N_HEADS = 8
D_ATTN = 512
D_GMLP = 512
N_GROUPS = 8
CHUNK = 128
ROPE_THETA = 500000.0
ROPE_DIM = 16
LN_EPS = 1e-5
NEG_INF = -1e30
WINDOW_BLOCK = 128
LANES = 128
N_PAIRS = D_ATTN // LANES
N_RES = 16
BLOCKS_PER_STEP = 8
LOOKAHEAD = 4
VMEM_LIMIT = 56 * 1024 * 1024

_INV_SQRT2 = 0.7071067811865476


def _gelu(t):
    return 0.5 * t * (1.0 + lax.erf(t * _INV_SQRT2))


def _layernorm_rows(t, g, b):
    mu = jnp.mean(t, axis=-1, keepdims=True)
    d = t - mu
    var = jnp.mean(d * d, axis=-1, keepdims=True)
    return d * lax.rsqrt(var + LN_EPS) * g + b


def _proj_kernel(x_ref, pos_ref, inv_ref, wqkv_ref, wuzT_ref, lnzg_ref, lnzb_ref, wsT_ref, bs_ref,
                 q_ref, k_ref, v_ref, gm_ref, nat_s, mid_s, *, tm, ts):
    subs = [(s * ts, ts) for s in range(tm // ts)]
    f32, bf = jnp.float32, jnp.bfloat16
    xb = [x_ref[r0:r0 + n, :].astype(bf) for (r0, n) in subs]
    h_t = [lax.dot_general(wuzT_ref[...], xb[s], (((1,), (1,)), ((), ())), preferred_element_type=f32)
           for s in range(len(subs))]
    h = [jnp.dot(xb[s], wqkv_ref[...], preferred_element_type=f32) for s in range(len(subs))]

    row = lax.broadcasted_iota(jnp.int32, (CHUNK, CHUNK), 0)
    col = lax.broadcasted_iota(jnp.int32, (CHUNK, CHUNK), 1)
    keep = row <= col
    w_sp = [jnp.where(keep, wsT_ref[g], 0.0).astype(bf) for g in range(N_GROUPS)]
    n_chunks = ts // CHUNK
    for s, (t0, n) in enumerate(subs):
        u_t = _gelu(h_t[s][:D_GMLP])
        z_t = _gelu(h_t[s][D_GMLP:])
        mu = jnp.mean(z_t, axis=0, keepdims=True)
        d = z_t - mu
        var = jnp.mean(d * d, axis=0, keepdims=True)
        zn_t = (d * lax.rsqrt(var + LN_EPS) * lnzg_ref[...] + lnzb_ref[...]).astype(bf)
        mixed = []
        for g in range(N_GROUPS):
            r0 = g * HEAD_DIM
            lhs = jnp.concatenate([zn_t[r0:r0 + HEAD_DIM, c * CHUNK:(c + 1) * CHUNK] for c in range(n_chunks)],
                                  axis=0)
            res = jnp.dot(lhs, w_sp[g], preferred_element_type=f32) + bs_ref[g:g + 1, :]
            mixed.append(jnp.concatenate([res[c * HEAD_DIM:(c + 1) * HEAD_DIM] for c in range(n_chunks)],
                                         axis=1))
        mixed_t = jnp.concatenate(mixed, axis=0)
        gm_ref[t0:t0 + n, :] = (u_t * mixed_t).T.astype(bf)

    pos = pos_ref[...].astype(f32)
    ang = pos * inv_ref[...]
    cos_t = jnp.cos(ang)
    sin_t = jnp.sin(ang)
    ones48 = jnp.ones((HEAD_DIM - ROPE_DIM, tm), f32)
    zeros48 = jnp.zeros((HEAD_DIM - ROPE_DIM, tm), f32)
    c_tab = jnp.concatenate([cos_t, cos_t, ones48, cos_t, cos_t, ones48], axis=0).T
    s_tab = jnp.concatenate([-sin_t, sin_t, zeros48, -sin_t, sin_t, zeros48], axis=0).T
    lane = lax.broadcasted_iota(jnp.int32, (1, LANES), 1)
    first_half = (lane % HEAD_DIM) < (ROPE_DIM // 2)

    def rope(t, t0, n):
        up = pltpu.roll(t, LANES - ROPE_DIM // 2, 1)
        dn = pltpu.roll(t, ROPE_DIM // 2, 1)
        return t * c_tab[t0:t0 + n] + jnp.where(first_half, up, dn) * s_tab[t0:t0 + n]

    scale = math.log2(math.e) / math.sqrt(HEAD_DIM)
    for s, (t0, n) in enumerate(subs):
        for hp in range(N_PAIRS):
            lo = hp * LANES
            nat_s[hp, t0:t0 + n, :] = rope(h[s][:, lo:lo + LANES], t0, n) * scale
            nat_s[N_PAIRS + hp, t0:t0 + n, :] = rope(h[s][:, D_ATTN + lo:D_ATTN + lo + LANES], t0, n)
            nat_s[2 * N_PAIRS + hp, t0:t0 + n, :] = h[s][:, 2 * D_ATTN + lo:2 * D_ATTN + lo + LANES]

    quarter = tm // 4
    n_l = tm // N_RES
    for idx in range(3 * N_PAIRS):
        for r in range(4):
            mid_s[idx, r * quarter:(r + 1) * quarter, :] = nat_s[idx, pl.ds(r, quarter, stride=4), :]
    for which, o_ref in enumerate((q_ref, k_ref, v_ref)):
        for hp in range(N_PAIRS):
            idx = which * N_PAIRS + hp
            for r_lo in range(4):
                for r_hi in range(4):
                    o_ref[hp, 4 * r_hi + r_lo] = mid_s[idx, pl.ds(r_lo * quarter + r_hi, n_l, stride=4), :]


def _branch_masks():
    w = WINDOW_BLOCK
    out = np.zeros((6, w, 2 * w), np.float32)
    rq, i8 = np.divmod(np.arange(w), 8)
    rk, j16 = np.divmod(np.arange(2 * w), 16)
    base = 16 * (i8[:, None] - j16[None, :]) + (rq[:, None] - rk[None, :])
    aq, i32 = np.divmod(np.arange(w), 32)
    ak, j64 = np.divmod(np.arange(2 * w), 64)
    base4 = 4 * (i32[:, None] - j64[None, :]) + (aq[:, None] - ak[None, :])
    base16 = np.arange(w)[:, None] - np.arange(2 * w)[None, :]
    for c, b in enumerate((base, base4, base16)):
        dist = b + w
        out[2 * c] = (dist >= 0) & (dist <= w)
        dist0 = b
        out[2 * c + 1] = (dist0 >= 0) & (dist0 <= w)
    return out


def _attn_kernel(q_s, k_s, v_s, mask_ref, o_ref, acc_s, m_s, l_s):
    n_l = q_s.shape[1]
    n_spans = n_l // WINDOW_BLOCK

    lane = lax.broadcasted_iota(jnp.int32, (1, LANES), 1)
    head0 = lane < HEAD_DIM

    def gather(ref, chunks):
        return jnp.concatenate([ref[r, pl.ds(st, n), :] for (r, st, n) in chunks], axis=0)

    def for_blocks(chains, first, last):
        w = WINDOW_BLOCK
        m_olds = None if first else [gather(m_s, ch[2]) for ch in chains]

        def scores(c):
            q_chunks, kv_chunks, _, _ = chains[c]
            qv = gather(q_s, q_chunks)
            kb = gather(k_s, kv_chunks).astype(jnp.bfloat16)
            q2 = jnp.concatenate([jnp.where(head0, qv, 0.0), jnp.where(head0, 0.0, qv)],
                                 axis=0).astype(jnp.bfloat16)
            return lax.dot_general(q2, kb, (((1,), (1,)), ((), ())),
                                   preferred_element_type=jnp.float32)

        def finish(c, s2):
            q_chunks, kv_chunks, st_chunks, mask = chains[c]
            valid = mask > 0.5
            m_old = None if first else m_olds[c]
            ps, ms, ls = [], [], []
            for hh in range(2):
                s = jnp.where(valid, s2[hh * w:(hh + 1) * w], NEG_INF)
                t = jnp.maximum(s[:, :LANES], s[:, LANES:])
                if not first:
                    own = head0 if hh == 0 else jnp.logical_not(head0)
                    t = jnp.maximum(t, jnp.where(own, m_old, NEG_INF))
                m_new = jnp.max(t, axis=1, keepdims=True)
                p = jnp.exp2(s - m_new)
                ps.append(p)
                ms.append(m_new)
                ls.append(jnp.sum(p, axis=1, keepdims=True))
            p2 = jnp.concatenate(ps, axis=0).astype(jnp.bfloat16)
            m_b = jnp.where(head0, ms[0], ms[1])
            l_b = jnp.where(head0, ls[0], ls[1])
            vb = gather(v_s, kv_chunks).astype(jnp.bfloat16)
            o2 = jnp.dot(p2, vb, preferred_element_type=jnp.float32)
            o_pair = jnp.where(head0, o2[:w], o2[w:])
            if first:
                acc, l_tot = o_pair, l_b
            else:
                alpha = jnp.exp2(m_old - m_b)
                acc = gather(acc_s, st_chunks) * alpha + o_pair
                l_tot = gather(l_s, st_chunks) * alpha + l_b
            if last:
                (r, st, n), = q_chunks
                o_ref[r, pl.ds(st, n), :] = (acc / l_tot).astype(o_ref.dtype)
            else:
                off = 0
                for (r, st, n) in st_chunks:
                    acc_s[r, pl.ds(st, n), :] = acc[off:off + n]
                    m_s[r, pl.ds(st, n), :] = m_b[off:off + n]
                    l_s[r, pl.ds(st, n), :] = l_tot[off:off + n]
                    off += n

        pending = {c: scores(c) for c in range(min(LOOKAHEAD, len(chains)))}
        for c in range(len(chains)):
            s2 = pending.pop(c)
            if c + LOOKAHEAD < len(chains):
                pending[c + LOOKAHEAD] = scores(c + LOOKAHEAD)
            finish(c, s2)

    for span in range(n_spans):
        def d1_block(n, span=span):
            nb = span * 16 + n
            q0 = pl.multiple_of(nb * 8, 8)
            k0 = pl.multiple_of(jnp.maximum(nb - 1, 0) * 8, 8)
            s0 = pl.multiple_of(n * 8, 8)
            return ([(r, q0, 8) for r in range(N_RES)], [(r, k0, 16) for r in range(N_RES)],
                    [(r, s0, 8) for r in range(N_RES)], mask_ref[jnp.where(nb == 0, 1, 0)])

        def d1_body(it, carry, d1_block=d1_block):
            for_blocks([d1_block(BLOCKS_PER_STEP * it + j) for j in range(BLOCKS_PER_STEP)], True, False)
            return carry
        lax.fori_loop(0, 16 // BLOCKS_PER_STEP, d1_body, 0)

        def d4_block(idx, span=span):
            r4 = idx // 4
            n = span * 4 + idx % 4
            q0 = pl.multiple_of(n * 32, 32)
            k0 = pl.multiple_of(jnp.maximum(n - 1, 0) * 32, 32)
            s0 = pl.multiple_of((idx % 4) * 32, 32)
            return ([(r4 + 4 * a, q0, 32) for a in range(4)], [(r4 + 4 * a, k0, 64) for a in range(4)],
                    [(r4 + 4 * a, s0, 32) for a in range(4)], mask_ref[2 + jnp.where(n == 0, 1, 0)])

        def d4_body(it, carry, d4_block=d4_block):
            for_blocks([d4_block(BLOCKS_PER_STEP * it + j) for j in range(BLOCKS_PER_STEP)], False, False)
            return carry
        lax.fori_loop(0, 16 // BLOCKS_PER_STEP, d4_body, 0)

        def d16_block(r16, span=span):
            q0 = span * WINDOW_BLOCK
            k0 = max(span - 1, 0) * WINDOW_BLOCK
            return ([(r16, q0, WINDOW_BLOCK)], [(r16, k0, 2 * WINDOW_BLOCK)],
                    [(r16, 0, WINDOW_BLOCK)], mask_ref[4 + (1 if span == 0 else 0)])

        def d16_body(it, carry, d16_block=d16_block):
            for_blocks([d16_block(BLOCKS_PER_STEP * it + j) for j in range(BLOCKS_PER_STEP)], False, True)
            return carry
        lax.fori_loop(0, N_RES // BLOCKS_PER_STEP, d16_body, 0)


def _tail_kernel(attn_ref, gm_ref, x_ref, p_ref, wo_ref, ln1g_ref, ln1b_ref, wa_ref, wb_ref, cw_ref, cb_ref,
                 wd_ref, ln2g_ref, ln2b_ref, wg_ref, bg_ref, wp_ref, ln3g_ref, ln3b_ref, o_ref, a_s, un_s,
                 *, tm, ts, tiles_per_seq, alpha):
    i = pl.program_id(0)

    @pl.when(i % tiles_per_seq == 0)
    def _():
        a_s[0:8, :] = jnp.zeros((8, a_s.shape[1]), jnp.float32)

    n_l = tm // N_RES
    for hp in range(N_PAIRS):
        for r in range(N_RES):
            un_s[hp, pl.ds(r, n_l, stride=N_RES), :] = attn_ref[hp, r].astype(jnp.float32)

    subs = [(s * ts, ts) for s in range(tm // ts)]
    f32, bf = jnp.float32, jnp.bfloat16
    mm = lambda a, w_ref: jnp.dot(a, w_ref[...], preferred_element_type=f32)

    mix = []
    for (r0, n) in subs:
        mixin = jnp.concatenate([un_s[hp, r0:r0 + n, :].astype(bf) for hp in range(N_PAIRS)]
                                + [gm_ref[r0:r0 + n, :]], axis=1)
        mix.append(mm(mixin, wo_ref))
    pe = [mm(p_ref[r0:r0 + n, :].astype(bf), wp_ref) for (r0, n) in subs]
    x1 = [_layernorm_rows(alpha * x_ref[r0:r0 + n, :] + mix[s], ln1g_ref[...], ln1b_ref[...])
          for s, (r0, n) in enumerate(subs)]
    x1b = [t.astype(bf) for t in x1]
    hb = []
    for s, (r0, n) in enumerate(subs):
        a_s[8 + r0:8 + r0 + n, :] = mm(x1b[s], wa_ref)
        hb.append(mm(x1b[s], wb_ref))
    ff = []
    for s, (r0, n) in enumerate(subs):
        conv = cb_ref[...] + cw_ref[0:1, :] * a_s[6 + r0:6 + r0 + n, :]
        conv = conv + cw_ref[1:2, :] * a_s[7 + r0:7 + r0 + n, :]
        conv = conv + cw_ref[2:3, :] * a_s[8 + r0:8 + r0 + n, :]
        ff.append(mm((_gelu(conv) * hb[s]).astype(bf), wd_ref))
    a_s[0:8, :] = a_s[tm:tm + 8, :]
    x2 = [_layernorm_rows(alpha * x1[s] + ff[s], ln2g_ref[...], ln2b_ref[...]) for s in range(len(subs))]
    gate = [jax.nn.sigmoid(mm(x2[s].astype(bf), wg_ref) + bg_ref[...]) for s in range(len(subs))]
    for s, (r0, n) in enumerate(subs):
        o_ref[r0:r0 + n, :] = _layernorm_rows(alpha * x2[s] + gate[s] * pe[s], ln3g_ref[...], ln3b_ref[...])


def _const_spec(shape):
    nd = len(shape)
    return pl.BlockSpec(shape, lambda *_: (0,) * nd, pipeline_mode=pl.Buffered(1))


def _layer(x, p, positions, w_in, ln_z_g, ln_z_b, w_s, b_s, w_o, ln1_g, ln1_b, w_ff_a, w_ff_b, conv_w, conv_b,
           w_ff_down, ln2_g, ln2_b, w_ple_gate, b_ple_gate, w_ple_in, ln3_g, ln3_b, *, alpha, tm1, ts1, tm3, ts3):
    B, S, D = x.shape
    N = B * S
    F = w_ff_a.shape[1]
    bf = jnp.bfloat16
    xf = x.reshape(N, D)

    inv = np.float32(ROPE_THETA ** (-np.arange(0, ROPE_DIM, 2, dtype=np.float64) / ROPE_DIM)).reshape(8, 1)
    wqkv = w_in[:, :3 * D_ATTN].astype(bf)
    wuzT = w_in[:, 3 * D_ATTN:].T.astype(bf)
    pos3 = positions.reshape(N // tm1, 1, tm1)
    tps1 = S // tm1
    n_l = S // N_RES
    row = lambda v: v.reshape(1, -1)
    q, k, v, gm = pl.pallas_call(
        functools.partial(_proj_kernel, tm=tm1, ts=ts1),
        grid=(N // tm1,),
        in_specs=[
            pl.BlockSpec((tm1, D), lambda i: (i, 0)),
            pl.BlockSpec((None, 1, tm1), lambda i: (i, 0, 0)),
            _const_spec((8, 1)),
            _const_spec((D, 3 * D_ATTN)),
            _const_spec((2 * D_GMLP, D)),
            _const_spec((D_GMLP, 1)),
            _const_spec((D_GMLP, 1)),
            _const_spec((N_GROUPS, CHUNK, CHUNK)),
            _const_spec((N_GROUPS, CHUNK)),
        ],
        out_specs=[pl.BlockSpec((N_PAIRS, None, N_RES, tm1 // N_RES, LANES),
                                lambda i: (0, i // tps1, 0, i % tps1, 0))] * 3
        + [pl.BlockSpec((tm1, D_GMLP), lambda i: (i, 0))],
        out_shape=[jax.ShapeDtypeStruct((N_PAIRS, B, N_RES, n_l, LANES), jnp.float32)] * 3
        + [jax.ShapeDtypeStruct((N, D_GMLP), bf)],
        scratch_shapes=[pltpu.VMEM((3 * N_PAIRS, tm1, LANES), jnp.float32)] * 2,
        compiler_params=pltpu.CompilerParams(dimension_semantics=("arbitrary",), vmem_limit_bytes=VMEM_LIMIT),
        name="proj",
    )(xf, pos3, jnp.asarray(inv), wqkv, wuzT, ln_z_g.reshape(-1, 1), ln_z_b.reshape(-1, 1),
      jnp.swapaxes(w_s, 1, 2), b_s)

    masks = jnp.asarray(_branch_masks())
    seq_spec = pl.BlockSpec((None, None, N_RES, n_l, LANES), lambda b, hp: (hp, b, 0, 0, 0))
    state = pltpu.VMEM((N_RES, WINDOW_BLOCK, LANES), jnp.float32)
    attn = pl.pallas_call(
        _attn_kernel,
        grid=(B, N_PAIRS),
        in_specs=[seq_spec, seq_spec, seq_spec, _const_spec(masks.shape)],
        out_specs=seq_spec,
        out_shape=jax.ShapeDtypeStruct((N_PAIRS, B, N_RES, n_l, LANES), bf),
        scratch_shapes=[state] * 3,
        compiler_params=pltpu.CompilerParams(dimension_semantics=("arbitrary", "arbitrary"),
                                             vmem_limit_bytes=VMEM_LIMIT),
        name="attn",
    )(q, k, v, masks)

    Dp = p.shape[-1]
    tps3 = S // tm3
    out = pl.pallas_call(
        functools.partial(_tail_kernel, tm=tm3, ts=ts3, tiles_per_seq=tps3, alpha=alpha),
        grid=(N // tm3,),
        in_specs=[
            pl.BlockSpec((N_PAIRS, None, N_RES, tm3 // N_RES, LANES), lambda i: (0, i // tps3, 0, i % tps3, 0)),
            pl.BlockSpec((tm3, D_GMLP), lambda i: (i, 0)),
            pl.BlockSpec((tm3, D), lambda i: (i, 0)),
            pl.BlockSpec((tm3, Dp), lambda i: (i, 0)),
            _const_spec((D, D)), _const_spec((1, D)), _const_spec((1, D)),
            _const_spec((D, F)), _const_spec((D, F)), _const_spec((3, F)), _const_spec((1, F)),
            _const_spec((F, D)), _const_spec((1, D)), _const_spec((1, D)),
            _const_spec((D, D)), _const_spec((1, D)), _const_spec((Dp, D)), _const_spec((1, D)), _const_spec((1, D)),
        ],
        out_specs=pl.BlockSpec((tm3, D), lambda i: (i, 0)),
        out_shape=jax.ShapeDtypeStruct((N, D), jnp.float32),
        scratch_shapes=[pltpu.VMEM((tm3 + 8, F), jnp.float32), pltpu.VMEM((N_PAIRS, tm3, LANES), jnp.float32)],
        compiler_params=pltpu.CompilerParams(dimension_semantics=("arbitrary",), vmem_limit_bytes=VMEM_LIMIT),
        name="tail",
    )(attn, gm, xf, p.reshape(N, Dp), w_o.astype(bf), row(ln1_g), row(ln1_b),
      w_ff_a.astype(bf), w_ff_b.astype(bf), conv_w, row(conv_b), w_ff_down.astype(bf), row(ln2_g), row(ln2_b),
      w_ple_gate.astype(bf), row(b_ple_gate), w_ple_in.astype(bf), row(ln3_g), row(ln3_b))
    return out.reshape(B, S, D)


def kernel(x, p, positions, w_in, ln_z_g, ln_z_b, w_s, b_s, w_o, ln1_g, ln1_b, w_ff_a, w_ff_b, conv_w, conv_b,
           w_ff_down, ln2_g, ln2_b, w_ple_gate, b_ple_gate, w_ple_in, ln3_g, ln3_b):
    depth = w_in.shape[0]
    alpha = (2.0 * depth) ** 0.25
    for i in range(depth):
        x = _layer(x, p[i], positions, w_in[i], ln_z_g[i], ln_z_b[i], w_s[i], b_s[i], w_o[i], ln1_g[i], ln1_b[i],
                   w_ff_a[i], w_ff_b[i], conv_w[i], conv_b[i], w_ff_down[i], ln2_g[i], ln2_b[i],
                   w_ple_gate[i], b_ple_gate[i], w_ple_in[i], ln3_g[i], ln3_b[i],
                   alpha=alpha, tm1=512, ts1=256, tm3=512, ts3=256)
    return x
```

```python
import functools
import math

import numpy as np
import jax
import jax.numpy as jnp
from jax import lax
from jax.experimental import pallas as pl
from jax.experimental.pallas import tpu as pltpu

HEAD_DIM = 64
N_HEADS = 8
D_ATTN = 512
D_GMLP = 512
N_GROUPS = 8
CHUNK = 128
ROPE_THETA = 500000.0
ROPE_DIM = 16
LN_EPS = 1e-5
NEG_INF = -1e30
WINDOW_BLOCK = 128
LANES = 128
N_PAIRS = D_ATTN // LANES
N_RES = 16
BLOCKS_PER_STEP = 8
CAST_STEPS = 16
LOOKAHEAD = 4
VMEM_LIMIT = 56 * 1024 * 1024

_INV_SQRT2 = 0.7071067811865476


def _gelu(t):
    return 0.5 * t * (1.0 + lax.erf(t * _INV_SQRT2))


def _layernorm_rows(t, g, b):
    mu = jnp.mean(t, axis=-1, keepdims=True)
    d = t - mu
    var = jnp.mean(d * d, axis=-1, keepdims=True)
    return d * lax.rsqrt(var + LN_EPS) * g + b


def _proj_kernel(x_ref, pos_ref, inv_ref, win_ref, lnzg_ref, lnzb_ref, ws_ref, bs_ref, *rest, tm, ts, n_cast):
    cast_in, rest = rest[:n_cast], rest[n_cast:]
    q_ref, k_ref, v_ref, gm_ref = rest[:4]
    cast_out = rest[4:4 + n_cast]
    wqkv_ref, wuzT_ref, nat_s, mid_s = rest[4 + n_cast:]
    f32, bf = jnp.float32, jnp.bfloat16
    i = pl.program_id(0)

    @pl.when(i == 0)
    def _():
        wqkv_ref[...] = win_ref[:, :3 * D_ATTN].astype(bf)
        wuzT_ref[...] = win_ref[:, 3 * D_ATTN:].T.astype(bf)

    @pl.when(i < CAST_STEPS)
    def _():
        for src, dst in zip(cast_in, cast_out):
            dst[...] = src[...].astype(bf)

    subs = [(s * ts, ts) for s in range(tm // ts)]
    xb = [x_ref[r0:r0 + n, :].astype(bf) for (r0, n) in subs]
    h_t = [lax.dot_general(wuzT_ref[...], xb[s], (((1,), (1,)), ((), ())), preferred_element_type=f32)
           for s in range(len(subs))]
    h = [jnp.dot(xb[s], wqkv_ref[...], preferred_element_type=f32) for s in range(len(subs))]

    row = lax.broadcasted_iota(jnp.int32, (CHUNK, CHUNK), 0)
    col = lax.broadcasted_iota(jnp.int32, (CHUNK, CHUNK), 1)
    keep = col <= row
    w_sp = [jnp.where(keep, ws_ref[g], 0.0).astype(bf) for g in range(N_GROUPS)]
    n_chunks = ts // CHUNK
    for s, (t0, n) in enumerate(subs):
        u_t = _gelu(h_t[s][:D_GMLP])
        z_t = _gelu(h_t[s][D_GMLP:])
        mu = jnp.mean(z_t, axis=0, keepdims=True)
        d = z_t - mu
        var = jnp.mean(d * d, axis=0, keepdims=True)
        zn_t = (d * lax.rsqrt(var + LN_EPS) * lnzg_ref[...] + lnzb_ref[...]).astype(bf)
        mixed = []
        for g in range(N_GROUPS):
            r0 = g * HEAD_DIM
            lhs = jnp.concatenate([zn_t[r0:r0 + HEAD_DIM, c * CHUNK:(c + 1) * CHUNK] for c in range(n_chunks)],
                                  axis=0)
            res = lax.dot_general(lhs, w_sp[g], (((1,), (1,)), ((), ())),
                                  preferred_element_type=f32) + bs_ref[g:g + 1, :]
            mixed.append(jnp.concatenate([res[c * HEAD_DIM:(c + 1) * HEAD_DIM] for c in range(n_chunks)],
                                         axis=1))
        mixed_t = jnp.concatenate(mixed, axis=0)
        gm_ref[t0:t0 + n, :] = (u_t * mixed_t).T.astype(bf)

    pos = pos_ref[...].astype(f32)
    ang = pos * inv_ref[...]
    cos_t = jnp.cos(ang)
    sin_t = jnp.sin(ang)
    ones48 = jnp.ones((HEAD_DIM - ROPE_DIM, tm), f32)
    zeros48 = jnp.zeros((HEAD_DIM - ROPE_DIM, tm), f32)
    c_tab = jnp.concatenate([cos_t, cos_t, ones48, cos_t, cos_t, ones48], axis=0).T
    s_tab = jnp.concatenate([-sin_t, sin_t, zeros48, -sin_t, sin_t, zeros48], axis=0).T
    lane = lax.broadcasted_iota(jnp.int32, (1, LANES), 1)
    first_half = (lane % HEAD_DIM) < (ROPE_DIM // 2)

    def rope(t, t0, n):
        up = pltpu.roll(t, LANES - ROPE_DIM // 2, 1)
        dn = pltpu.roll(t, ROPE_DIM // 2, 1)
        return t * c_tab[t0:t0 + n] + jnp.where(first_half, up, dn) * s_tab[t0:t0 + n]

    scale = math.log2(math.e) / math.sqrt(HEAD_DIM)
    for s, (t0, n) in enumerate(subs):
        for hp in range(N_PAIRS):
            lo = hp * LANES
            nat_s[hp, t0:t0 + n, :] = rope(h[s][:, lo:lo + LANES], t0, n) * scale
            nat_s[N_PAIRS + hp, t0:t0 + n, :] = rope(h[s][:, D_ATTN + lo:D_ATTN + lo + LANES], t0, n)
            nat_s[2 * N_PAIRS + hp, t0:t0 + n, :] = h[s][:, 2 * D_ATTN + lo:2 * D_ATTN + lo + LANES]

    quarter = tm // 4
    n_l = tm // N_RES
    for idx in range(3 * N_PAIRS):
        for r in range(4):
            mid_s[idx, r * quarter:(r + 1) * quarter, :] = nat_s[idx, pl.ds(r, quarter, stride=4), :]
    for which, o_ref in enumerate((q_ref, k_ref, v_ref)):
        for hp in range(N_PAIRS):
            idx = which * N_PAIRS + hp
            for r_lo in range(4):
                for r_hi in range(4):
                    o_ref[hp, 4 * r_hi + r_lo] = mid_s[idx, pl.ds(r_lo * quarter + r_hi, n_l, stride=4), :]


def _branch_masks():
    w = WINDOW_BLOCK
    out = np.zeros((6, w, 2 * w), np.float32)
    rq, i8 = np.divmod(np.arange(w), 8)
    rk, j16 = np.divmod(np.arange(2 * w), 16)
    base = 16 * (i8[:, None] - j16[None, :]) + (rq[:, None] - rk[None, :])
    aq, i32 = np.divmod(np.arange(w), 32)
    ak, j64 = np.divmod(np.arange(2 * w), 64)
    base4 = 4 * (i32[:, None] - j64[None, :]) + (aq[:, None] - ak[None, :])
    base16 = np.arange(w)[:, None] - np.arange(2 * w)[None, :]
    for c, b in enumerate((base, base4, base16)):
        dist = b + w
        out[2 * c] = (dist >= 0) & (dist <= w)
        dist0 = b
        out[2 * c + 1] = (dist0 >= 0) & (dist0 <= w)
    return out


def _attn_kernel(q_s, k_s, v_s, mask_ref, o_ref, acc_s, m_s, l_s):
    n_l = q_s.shape[1]
    n_spans = n_l // WINDOW_BLOCK

    lane = lax.broadcasted_iota(jnp.int32, (1, LANES), 1)
    head0 = lane < HEAD_DIM

    def gather(ref, chunks):
        return jnp.concatenate([ref[r, pl.ds(st, n), :] for (r, st, n) in chunks], axis=0)

    def for_blocks(chains, first, last):
        w = WINDOW_BLOCK
        m_olds = None if first else [gather(m_s, ch[2]) for ch in chains]

        def scores(c):
            q_chunks, kv_chunks, _, _ = chains[c]
            qv = gather(q_s, q_chunks)
            kb = gather(k_s, kv_chunks).astype(jnp.bfloat16)
            q2 = jnp.concatenate([jnp.where(head0, qv, 0.0), jnp.where(head0, 0.0, qv)],
                                 axis=0).astype(jnp.bfloat16)
            return lax.dot_general(q2, kb, (((1,), (1,)), ((), ())),
                                   preferred_element_type=jnp.float32)

        def finish(c, s2):
            q_chunks, kv_chunks, st_chunks, mask = chains[c]
            valid = mask > 0.5
            m_old = None if first else m_olds[c]
            ps, ms, ls = [], [], []
            for hh in range(2):
                s = jnp.where(valid, s2[hh * w:(hh + 1) * w], NEG_INF)
                t = jnp.maximum(s[:, :LANES], s[:, LANES:])
                if not first:
                    own = head0 if hh == 0 else jnp.logical_not(head0)
                    t = jnp.maximum(t, jnp.where(own, m_old, NEG_INF))
                m_new = jnp.max(t, axis=1, keepdims=True)
                p = jnp.exp2(s - m_new)
                ps.append(p)
                ms.append(m_new)
                ls.append(jnp.sum(p, axis=1, keepdims=True))
            p2 = jnp.concatenate(ps, axis=0).astype(jnp.bfloat16)
            m_b = jnp.where(head0, ms[0], ms[1])
            l_b = jnp.where(head0, ls[0], ls[1])
            vb = gather(v_s, kv_chunks).astype(jnp.bfloat16)
            o2 = jnp.dot(p2, vb, preferred_element_type=jnp.float32)
            o_pair = jnp.where(head0, o2[:w], o2[w:])
            if first:
                acc, l_tot = o_pair, l_b
            else:
                alpha = jnp.exp2(m_old - m_b)
                acc = gather(acc_s, st_chunks) * alpha + o_pair
                l_tot = gather(l_s, st_chunks) * alpha + l_b
            if last:
                (r, st, n), = q_chunks
                o_ref[r, pl.ds(st, n), :] = (acc / l_tot).astype(o_ref.dtype)
            else:
                off = 0
                for (r, st, n) in st_chunks:
                    acc_s[r, pl.ds(st, n), :] = acc[off:off + n]
                    m_s[r, pl.ds(st, n), :] = m_b[off:off + n]
                    l_s[r, pl.ds(st, n), :] = l_tot[off:off + n]
                    off += n

        pending = {c: scores(c) for c in range(min(LOOKAHEAD, len(chains)))}
        for c in range(len(chains)):
            s2 = pending.pop(c)
            if c + LOOKAHEAD < len(chains):
                pending[c + LOOKAHEAD] = scores(c + LOOKAHEAD)
            finish(c, s2)

    for span in range(n_spans):
        def d1_block(n, span=span):
            nb = span * 16 + n
            q0 = pl.multiple_of(nb * 8, 8)
            k0 = pl.multiple_of(jnp.maximum(nb - 1, 0) * 8, 8)
            s0 = pl.multiple_of(n * 8, 8)
            return ([(r, q0, 8) for r in range(N_RES)], [(r, k0, 16) for r in range(N_RES)],
                    [(r, s0, 8) for r in range(N_RES)], mask_ref[jnp.where(nb == 0, 1, 0)])

        def d1_body(it, carry, d1_block=d1_block):
            for_blocks([d1_block(BLOCKS_PER_STEP * it + j) for j in range(BLOCKS_PER_STEP)], True, False)
            return carry
        lax.fori_loop(0, 16 // BLOCKS_PER_STEP, d1_body, 0)

        def d4_block(idx, span=span):
            r4 = idx // 4
            n = span * 4 + idx % 4
            q0 = pl.multiple_of(n * 32, 32)
            k0 = pl.multiple_of(jnp.maximum(n - 1, 0) * 32, 32)
            s0 = pl.multiple_of((idx % 4) * 32, 32)
            return ([(r4 + 4 * a, q0, 32) for a in range(4)], [(r4 + 4 * a, k0, 64) for a in range(4)],
                    [(r4 + 4 * a, s0, 32) for a in range(4)], mask_ref[2 + jnp.where(n == 0, 1, 0)])

        def d4_body(it, carry, d4_block=d4_block):
            for_blocks([d4_block(BLOCKS_PER_STEP * it + j) for j in range(BLOCKS_PER_STEP)], False, False)
            return carry
        lax.fori_loop(0, 16 // BLOCKS_PER_STEP, d4_body, 0)

        def d16_block(r16, span=span):
            q0 = span * WINDOW_BLOCK
            k0 = max(span - 1, 0) * WINDOW_BLOCK
            return ([(r16, q0, WINDOW_BLOCK)], [(r16, k0, 2 * WINDOW_BLOCK)],
                    [(r16, 0, WINDOW_BLOCK)], mask_ref[4 + (1 if span == 0 else 0)])

        def d16_body(it, carry, d16_block=d16_block):
            for_blocks([d16_block(BLOCKS_PER_STEP * it + j) for j in range(BLOCKS_PER_STEP)], False, True)
            return carry
        lax.fori_loop(0, N_RES // BLOCKS_PER_STEP, d16_body, 0)


def _tail_kernel(attn_ref, gm_ref, x_ref, p_ref, wo_ref, ln1g_ref, ln1b_ref, wa_ref, wb_ref, cw_ref, cb_ref,
                 wd_ref, ln2g_ref, ln2b_ref, wg_ref, bg_ref, wp_ref, ln3g_ref, ln3b_ref, o_ref, a_s, un_s,
                 *, tm, ts, tiles_per_seq, alpha):
    i = pl.program_id(0)

    @pl.when(i % tiles_per_seq == 0)
    def _():
        a_s[0:8, :] = jnp.zeros((8, a_s.shape[1]), jnp.float32)

    subs = [(s * ts, ts) for s in range(tm // ts)]
    f32, bf = jnp.float32, jnp.bfloat16
    mm = lambda a, w_ref: jnp.dot(a, w_ref[...], preferred_element_type=f32)

    pe = [mm(p_ref[r0:r0 + n, :].astype(bf), wp_ref) for (r0, n) in subs]

    n_l = tm // N_RES
    for hp in range(N_PAIRS):
        for r in range(N_RES):
            un_s[hp, pl.ds(r, n_l, stride=N_RES), :] = attn_ref[hp, r].astype(f32)

    mix = []
    for (r0, n) in subs:
        mixin = jnp.concatenate([un_s[hp, r0:r0 + n, :].astype(bf) for hp in range(N_PAIRS)]
                                + [gm_ref[r0:r0 + n, :]], axis=1)
        mix.append(mm(mixin, wo_ref))
    x1 = [_layernorm_rows(alpha * x_ref[r0:r0 + n, :] + mix[s], ln1g_ref[...], ln1b_ref[...])
          for s, (r0, n) in enumerate(subs)]
    x1b = [t.astype(bf) for t in x1]
    hb = []
    for s, (r0, n) in enumerate(subs):
        a_s[8 + r0:8 + r0 + n, :] = mm(x1b[s], wa_ref)
        hb.append(mm(x1b[s], wb_ref))
    ff = []
    for s, (r0, n) in enumerate(subs):
        conv = cb_ref[...] + cw_ref[0:1, :] * a_s[6 + r0:6 + r0 + n, :]
        conv = conv + cw_ref[1:2, :] * a_s[7 + r0:7 + r0 + n, :]
        conv = conv + cw_ref[2:3, :] * a_s[8 + r0:8 + r0 + n, :]
        ff.append(mm((_gelu(conv) * hb[s]).astype(bf), wd_ref))
    a_s[0:8, :] = a_s[tm:tm + 8, :]
    x2 = [_layernorm_rows(alpha * x1[s] + ff[s], ln2g_ref[...], ln2b_ref[...]) for s in range(len(subs))]
    gate = [jax.nn.sigmoid(mm(x2[s].astype(bf), wg_ref) + bg_ref[...]) for s in range(len(subs))]
    for s, (r0, n) in enumerate(subs):
        o_ref[r0:r0 + n, :] = _layernorm_rows(alpha * x2[s] + gate[s] * pe[s], ln3g_ref[...], ln3b_ref[...])


def _const_spec(shape):
    nd = len(shape)
    return pl.BlockSpec(shape, lambda *_: (0,) * nd, pipeline_mode=pl.Buffered(1))


def _layer(x, p, positions, w_in, ln_z_g, ln_z_b, w_s, b_s, w_o, ln1_g, ln1_b, w_ff_a, w_ff_b, conv_w, conv_b,
           w_ff_down, ln2_g, ln2_b, w_ple_gate, b_ple_gate, w_ple_in, ln3_g, ln3_b, *, alpha, tm1, ts1, tm3, ts3):
    B, S, D = x.shape
    N = B * S
    F = w_ff_a.shape[1]
    bf = jnp.bfloat16
    xf = x.reshape(N, D)

    inv = np.float32(ROPE_THETA ** (-np.arange(0, ROPE_DIM, 2, dtype=np.float64) / ROPE_DIM)).reshape(8, 1)
    pos3 = positions.reshape(N // tm1, 1, tm1)
    tps1 = S // tm1
    n_l = S // N_RES
    row = lambda v: v.reshape(1, -1)
    later_w = [w_o, w_ff_a, w_ff_b, w_ff_down, w_ple_gate, w_ple_in]
    slab = lambda w: pl.BlockSpec((w.shape[0] // CAST_STEPS, w.shape[1]),
                                  lambda i: (jnp.minimum(i, CAST_STEPS - 1), 0))
    q, k, v, gm, w_o_b, w_a_b, w_b_b, w_d_b, w_g_b, w_p_b = pl.pallas_call(
        functools.partial(_proj_kernel, tm=tm1, ts=ts1, n_cast=len(later_w)),
        grid=(N // tm1,),
        in_specs=[
            pl.BlockSpec((tm1, D), lambda i: (i, 0)),
            pl.BlockSpec((None, 1, tm1), lambda i: (i, 0, 0)),
            _const_spec((8, 1)),
            _const_spec(w_in.shape),
            _const_spec((D_GMLP, 1)),
            _const_spec((D_GMLP, 1)),
            _const_spec((N_GROUPS, CHUNK, CHUNK)),
            _const_spec((N_GROUPS, CHUNK)),
        ] + [slab(w) for w in later_w],
        out_specs=[pl.BlockSpec((N_PAIRS, None, N_RES, tm1 // N_RES, LANES),
                                lambda i: (0, i // tps1, 0, i % tps1, 0))] * 3
        + [pl.BlockSpec((tm1, D_GMLP), lambda i: (i, 0))] + [slab(w) for w in later_w],
        out_shape=[jax.ShapeDtypeStruct((N_PAIRS, B, N_RES, n_l, LANES), jnp.float32)] * 3
        + [jax.ShapeDtypeStruct((N, D_GMLP), bf)] + [jax.ShapeDtypeStruct(w.shape, bf) for w in later_w],
        scratch_shapes=[pltpu.VMEM((D, 3 * D_ATTN), bf), pltpu.VMEM((2 * D_GMLP, D), bf)]
        + [pltpu.VMEM((3 * N_PAIRS, tm1, LANES), jnp.float32)] * 2,
        compiler_params=pltpu.CompilerParams(dimension_semantics=("arbitrary",), vmem_limit_bytes=VMEM_LIMIT),
        name="proj",
    )(xf, pos3, jnp.asarray(inv), w_in, ln_z_g.reshape(-1, 1), ln_z_b.reshape(-1, 1), w_s, b_s, *later_w)

    masks = jnp.asarray(_branch_masks())
    seq_spec = pl.BlockSpec((None, None, N_RES, n_l, LANES), lambda b, hp: (hp, b, 0, 0, 0))
    state = pltpu.VMEM((N_RES, WINDOW_BLOCK, LANES), jnp.float32)
    attn = pl.pallas_call(
        _attn_kernel,
        grid=(B, N_PAIRS),
        in_specs=[seq_spec, seq_spec, seq_spec, _const_spec(masks.shape)],
        out_specs=seq_spec,
        out_shape=jax.ShapeDtypeStruct((N_PAIRS, B, N_RES, n_l, LANES), bf),
        scratch_shapes=[state] * 3,
        compiler_params=pltpu.CompilerParams(dimension_semantics=("arbitrary", "arbitrary"),
                                             vmem_limit_bytes=VMEM_LIMIT),
        name="attn",
    )(q, k, v, masks)

    Dp = p.shape[-1]
    tps3 = S // tm3
    out = pl.pallas_call(
        functools.partial(_tail_kernel, tm=tm3, ts=ts3, tiles_per_seq=tps3, alpha=alpha),
        grid=(N // tm3,),
        in_specs=[
            pl.BlockSpec((N_PAIRS, None, N_RES, tm3 // N_RES, LANES), lambda i: (0, i // tps3, 0, i % tps3, 0)),
            pl.BlockSpec((tm3, D_GMLP), lambda i: (i, 0)),
            pl.BlockSpec((tm3, D), lambda i: (i, 0)),
            pl.BlockSpec((tm3, Dp), lambda i: (i, 0)),
            _const_spec((D, D)), _const_spec((1, D)), _const_spec((1, D)),
            _const_spec((D, F)), _const_spec((D, F)), _const_spec((3, F)), _const_spec((1, F)),
            _const_spec((F, D)), _const_spec((1, D)), _const_spec((1, D)),
            _const_spec((D, D)), _const_spec((1, D)), _const_spec((Dp, D)), _const_spec((1, D)), _const_spec((1, D)),
        ],
        out_specs=pl.BlockSpec((tm3, D), lambda i: (i, 0)),
        out_shape=jax.ShapeDtypeStruct((N, D), jnp.float32),
        scratch_shapes=[pltpu.VMEM((tm3 + 8, F), jnp.float32), pltpu.VMEM((N_PAIRS, tm3, LANES), jnp.float32)],
        compiler_params=pltpu.CompilerParams(dimension_semantics=("arbitrary",), vmem_limit_bytes=VMEM_LIMIT),
        name="tail",
    )(attn, gm, xf, p.reshape(N, Dp), w_o_b, row(ln1_g), row(ln1_b),
      w_a_b, w_b_b, conv_w, row(conv_b), w_d_b, row(ln2_g), row(ln2_b),
      w_g_b, row(b_ple_gate), w_p_b, row(ln3_g), row(ln3_b))
    return out.reshape(B, S, D)


def kernel(x, p, positions, w_in, ln_z_g, ln_z_b, w_s, b_s, w_o, ln1_g, ln1_b, w_ff_a, w_ff_b, conv_w, conv_b,
           w_ff_down, ln2_g, ln2_b, w_ple_gate, b_ple_gate, w_ple_in, ln3_g, ln3_b):
    depth = w_in.shape[0]
    alpha = (2.0 * depth) ** 0.25
    for i in range(depth):
        x = _layer(x, p[i], positions, w_in[i], ln_z_g[i], ln_z_b[i], w_s[i], b_s[i], w_o[i], ln1_g[i], ln1_b[i],
                   w_ff_a[i], w_ff_b[i], conv_w[i], conv_b[i], w_ff_down[i], ln2_g[i], ln2_b[i],
                   w_ple_gate[i], b_ple_gate[i], w_ple_in[i], ln3_g[i], ln3_b[i],
                   alpha=alpha, tm1=512, ts1=256, tm3=512, ts3=256)
    return x
```

```python
import functools
import math

import numpy as np
import jax
import jax.numpy as jnp
from jax import lax
from jax.experimental import pallas as pl
from jax.experimental.pallas import tpu as pltpu

HEAD_DIM = 64
N_HEADS = 8
D_ATTN = 512
D_GMLP = 512
N_GROUPS = 8
CHUNK = 128
ROPE_THETA = 500000.0
ROPE_DIM = 16
LN_EPS = 1e-5
NEG_INF = -1e30
WINDOW_BLOCK = 128
LANES = 128
N_PAIRS = D_ATTN // LANES
N_RES = 16
BLOCKS_PER_STEP = 8
CAST_STEPS = 16
LOOKAHEAD = 4
VMEM_LIMIT = 56 * 1024 * 1024

_INV_SQRT2 = 0.7071067811865476


def _gelu(t):
    return 0.5 * t * (1.0 + lax.erf(t * _INV_SQRT2))


def _layernorm_rows(t, g, b):
    mu = jnp.mean(t, axis=-1, keepdims=True)
    d = t - mu
    var = jnp.mean(d * d, axis=-1, keepdims=True)
    return d * lax.rsqrt(var + LN_EPS) * g + b


def _proj_kernel(x_ref, pos_ref, inv_ref, win_ref, lnzg_ref, lnzb_ref, ws_ref, bs_ref, *rest, tm, ts, cast_scales):
    n_cast = len(cast_scales)
    cast_in, rest = rest[:n_cast], rest[n_cast:]
    q_ref, k_ref, v_ref, gm_ref = rest[:4]
    cast_out = rest[4:4 + n_cast]
    wqkv_ref, wuzT_ref, nat_s, mid_s = rest[4 + n_cast:]
    f32, bf = jnp.float32, jnp.bfloat16
    i = pl.program_id(0)

    @pl.when(i == 0)
    def _():
        wqkv_ref[...] = win_ref[:, :3 * D_ATTN].astype(bf)
        wuzT_ref[...] = win_ref[:, 3 * D_ATTN:].T.astype(bf)

    @pl.when(i < CAST_STEPS)
    def _():
        for src, dst, c in zip(cast_in, cast_out, cast_scales):
            dst[...] = (src[...] if c == 1.0 else src[...] * c).astype(bf)

    subs = [(s * ts, ts) for s in range(tm // ts)]
    xb = [x_ref[r0:r0 + n, :].astype(bf) for (r0, n) in subs]
    h_t = [lax.dot_general(wuzT_ref[...], xb[s], (((1,), (1,)), ((), ())), preferred_element_type=f32)
           for s in range(len(subs))]
    h = [jnp.dot(xb[s], wqkv_ref[...], preferred_element_type=f32) for s in range(len(subs))]

    row = lax.broadcasted_iota(jnp.int32, (CHUNK, CHUNK), 0)
    col = lax.broadcasted_iota(jnp.int32, (CHUNK, CHUNK), 1)
    keep = col <= row
    w_sp = [jnp.where(keep, ws_ref[g], 0.0).astype(bf) for g in range(N_GROUPS)]
    n_chunks = ts // CHUNK
    for s, (t0, n) in enumerate(subs):
        u_t = _gelu(h_t[s][:D_GMLP])
        z_t = _gelu(h_t[s][D_GMLP:])
        mu = jnp.mean(z_t, axis=0, keepdims=True)
        d = z_t - mu
        var = jnp.mean(d * d, axis=0, keepdims=True)
        zn_t = (d * lax.rsqrt(var + LN_EPS) * lnzg_ref[...] + lnzb_ref[...]).astype(bf)
        mixed = []
        for g in range(N_GROUPS):
            r0 = g * HEAD_DIM
            lhs = jnp.concatenate([zn_t[r0:r0 + HEAD_DIM, c * CHUNK:(c + 1) * CHUNK] for c in range(n_chunks)],
                                  axis=0)
            res = lax.dot_general(lhs, w_sp[g], (((1,), (1,)), ((), ())),
                                  preferred_element_type=f32) + bs_ref[g:g + 1, :]
            mixed.append(jnp.concatenate([res[c * HEAD_DIM:(c + 1) * HEAD_DIM] for c in range(n_chunks)],
                                         axis=1))
        mixed_t = jnp.concatenate(mixed, axis=0)
        gm_ref[t0:t0 + n, :] = (u_t * mixed_t).T.astype(bf)

    pos = pos_ref[...].astype(f32)
    ang = pos * inv_ref[...]
    cos_t = jnp.cos(ang)
    sin_t = jnp.sin(ang)
    ones48 = jnp.ones((HEAD_DIM - ROPE_DIM, tm), f32)
    zeros48 = jnp.zeros((HEAD_DIM - ROPE_DIM, tm), f32)
    c_tab = jnp.concatenate([cos_t, cos_t, ones48, cos_t, cos_t, ones48], axis=0).T
    s_tab = jnp.concatenate([-sin_t, sin_t, zeros48, -sin_t, sin_t, zeros48], axis=0).T
    lane = lax.broadcasted_iota(jnp.int32, (1, LANES), 1)
    first_half = (lane % HEAD_DIM) < (ROPE_DIM // 2)

    def rope(t, t0, n):
        up = pltpu.roll(t, LANES - ROPE_DIM // 2, 1)
        dn = pltpu.roll(t, ROPE_DIM // 2, 1)
        return t * c_tab[t0:t0 + n] + jnp.where(first_half, up, dn) * s_tab[t0:t0 + n]

    scale = math.log2(math.e) / math.sqrt(HEAD_DIM)
    for s, (t0, n) in enumerate(subs):
        for hp in range(N_PAIRS):
            lo = hp * LANES
            nat_s[hp, t0:t0 + n, :] = rope(h[s][:, lo:lo + LANES], t0, n) * scale
            nat_s[N_PAIRS + hp, t0:t0 + n, :] = rope(h[s][:, D_ATTN + lo:D_ATTN + lo + LANES], t0, n)
            nat_s[2 * N_PAIRS + hp, t0:t0 + n, :] = h[s][:, 2 * D_ATTN + lo:2 * D_ATTN + lo + LANES]

    quarter = tm // 4
    n_l = tm // N_RES
    for idx in range(3 * N_PAIRS):
        for r in range(4):
            mid_s[idx, r * quarter:(r + 1) * quarter, :] = nat_s[idx, pl.ds(r, quarter, stride=4), :]
    for which, o_ref in enumerate((q_ref, k_ref, v_ref)):
        for hp in range(N_PAIRS):
            idx = which * N_PAIRS + hp
            for r_lo in range(4):
                for r_hi in range(4):
                    o_ref[hp, 4 * r_hi + r_lo] = mid_s[idx, pl.ds(r_lo * quarter + r_hi, n_l, stride=4), :]


def _branch_masks():
    w = WINDOW_BLOCK
    out = np.zeros((6, w, 2 * w), np.float32)
    rq, i8 = np.divmod(np.arange(w), 8)
    rk, j16 = np.divmod(np.arange(2 * w), 16)
    base = 16 * (i8[:, None] - j16[None, :]) + (rq[:, None] - rk[None, :])
    aq, i32 = np.divmod(np.arange(w), 32)
    ak, j64 = np.divmod(np.arange(2 * w), 64)
    base4 = 4 * (i32[:, None] - j64[None, :]) + (aq[:, None] - ak[None, :])
    base16 = np.arange(w)[:, None] - np.arange(2 * w)[None, :]
    for c, b in enumerate((base, base4, base16)):
        dist = b + w
        out[2 * c] = (dist >= 0) & (dist <= w)
        dist0 = b
        out[2 * c + 1] = (dist0 >= 0) & (dist0 <= w)
    return out


def _attn_kernel(q_s, k_s, v_s, mask_ref, o_ref, acc_s, m_s, l_s):
    n_l = q_s.shape[1]
    n_spans = n_l // WINDOW_BLOCK

    lane = lax.broadcasted_iota(jnp.int32, (1, LANES), 1)
    head0 = lane < HEAD_DIM

    def gather(ref, chunks):
        return jnp.concatenate([ref[r, pl.ds(st, n), :] for (r, st, n) in chunks], axis=0)

    def for_blocks(chains, first, last):
        w = WINDOW_BLOCK
        m_olds = None if first else [gather(m_s, ch[2]) for ch in chains]

        def scores(c):
            q_chunks, kv_chunks, _, _ = chains[c]
            qv = gather(q_s, q_chunks)
            kb = gather(k_s, kv_chunks).astype(jnp.bfloat16)
            q2 = jnp.concatenate([jnp.where(head0, qv, 0.0), jnp.where(head0, 0.0, qv)],
                                 axis=0).astype(jnp.bfloat16)
            return lax.dot_general(q2, kb, (((1,), (1,)), ((), ())),
                                   preferred_element_type=jnp.float32)

        def finish(c, s2):
            q_chunks, kv_chunks, st_chunks, mask = chains[c]
            valid = mask > 0.5
            m_old = None if first else m_olds[c]
            ps, ms = [], []
            for hh in range(2):
                s = jnp.where(valid, s2[hh * w:(hh + 1) * w], NEG_INF)
                t = jnp.maximum(s[:, :LANES], s[:, LANES:])
                if not first:
                    own = head0 if hh == 0 else jnp.logical_not(head0)
                    t = jnp.maximum(t, jnp.where(own, m_old, NEG_INF))
                m_new = jnp.max(t, axis=1, keepdims=True)
                ps.append(jnp.exp2(s - m_new))
                ms.append(m_new)
            p2 = jnp.concatenate(ps, axis=0).astype(jnp.bfloat16)
            m_b = jnp.where(head0, ms[0], ms[1])
            vb = gather(v_s, kv_chunks).astype(jnp.bfloat16)
            vb1 = jnp.concatenate([vb, jnp.ones_like(vb)], axis=1)
            o2 = jnp.dot(p2, vb1, preferred_element_type=jnp.float32)
            o_pair = jnp.where(head0, o2[:w, :LANES], o2[w:, :LANES])
            l_b = jnp.where(head0, o2[:w, LANES:], o2[w:, LANES:])
            if first:
                acc, l_tot = o_pair, l_b
            else:
                alpha = jnp.exp2(m_old - m_b)
                acc = gather(acc_s, st_chunks) * alpha + o_pair
                l_tot = gather(l_s, st_chunks) * alpha + l_b
            if last:
                (r, st, n), = q_chunks
                o_ref[r, pl.ds(st, n), :] = (acc / l_tot).astype(o_ref.dtype)
            else:
                off = 0
                for (r, st, n) in st_chunks:
                    acc_s[r, pl.ds(st, n), :] = acc[off:off + n]
                    m_s[r, pl.ds(st, n), :] = m_b[off:off + n]
                    l_s[r, pl.ds(st, n), :] = l_tot[off:off + n]
                    off += n

        pending = {c: scores(c) for c in range(min(LOOKAHEAD, len(chains)))}
        for c in range(len(chains)):
            s2 = pending.pop(c)
            if c + LOOKAHEAD < len(chains):
                pending[c + LOOKAHEAD] = scores(c + LOOKAHEAD)
            finish(c, s2)

    for span in range(n_spans):
        def d1_block(n, span=span):
            nb = span * 16 + n
            q0 = pl.multiple_of(nb * 8, 8)
            k0 = pl.multiple_of(jnp.maximum(nb - 1, 0) * 8, 8)
            s0 = pl.multiple_of(n * 8, 8)
            return ([(r, q0, 8) for r in range(N_RES)], [(r, k0, 16) for r in range(N_RES)],
                    [(r, s0, 8) for r in range(N_RES)], mask_ref[jnp.where(nb == 0, 1, 0)])

        def d1_body(it, carry, d1_block=d1_block):
            for_blocks([d1_block(BLOCKS_PER_STEP * it + j) for j in range(BLOCKS_PER_STEP)], True, False)
            return carry
        lax.fori_loop(0, 16 // BLOCKS_PER_STEP, d1_body, 0)

        def d4_block(idx, span=span):
            r4 = idx // 4
            n = span * 4 + idx % 4
            q0 = pl.multiple_of(n * 32, 32)
            k0 = pl.multiple_of(jnp.maximum(n - 1, 0) * 32, 32)
            s0 = pl.multiple_of((idx % 4) * 32, 32)
            return ([(r4 + 4 * a, q0, 32) for a in range(4)], [(r4 + 4 * a, k0, 64) for a in range(4)],
                    [(r4 + 4 * a, s0, 32) for a in range(4)], mask_ref[2 + jnp.where(n == 0, 1, 0)])

        def d4_body(it, carry, d4_block=d4_block):
            for_blocks([d4_block(BLOCKS_PER_STEP * it + j) for j in range(BLOCKS_PER_STEP)], False, False)
            return carry
        lax.fori_loop(0, 16 // BLOCKS_PER_STEP, d4_body, 0)

        def d16_block(r16, span=span):
            q0 = span * WINDOW_BLOCK
            k0 = max(span - 1, 0) * WINDOW_BLOCK
            return ([(r16, q0, WINDOW_BLOCK)], [(r16, k0, 2 * WINDOW_BLOCK)],
                    [(r16, 0, WINDOW_BLOCK)], mask_ref[4 + (1 if span == 0 else 0)])

        def d16_body(it, carry, d16_block=d16_block):
            for_blocks([d16_block(BLOCKS_PER_STEP * it + j) for j in range(BLOCKS_PER_STEP)], False, True)
            return carry
        lax.fori_loop(0, N_RES // BLOCKS_PER_STEP, d16_body, 0)


def _tail_kernel(attn_ref, gm_ref, x_ref, p_ref, wo_ref, ln1g_ref, ln1b_ref, wa_ref, wb_ref, cw_ref, cb_ref,
                 wd_ref, ln2g_ref, ln2b_ref, wg_ref, bg_ref, wp_ref, ln3g_ref, ln3b_ref, o_ref, a_s, un_s,
                 *, tm, ts, tiles_per_seq, alpha):
    i = pl.program_id(0)

    @pl.when(i % tiles_per_seq == 0)
    def _():
        a_s[0:8, :] = jnp.zeros((8, a_s.shape[1]), jnp.float32)

    subs = [(s * ts, ts) for s in range(tm // ts)]
    f32, bf = jnp.float32, jnp.bfloat16
    mm = lambda a, w_ref: jnp.dot(a, w_ref[...], preferred_element_type=f32)

    pe = [mm(p_ref[r0:r0 + n, :].astype(bf), wp_ref) for (r0, n) in subs]

    n_l = tm // N_RES
    for hp in range(N_PAIRS):
        for r in range(N_RES):
            un_s[hp, pl.ds(r, n_l, stride=N_RES), :] = attn_ref[hp, r].astype(f32)

    mix = []
    for (r0, n) in subs:
        mixin = jnp.concatenate([un_s[hp, r0:r0 + n, :].astype(bf) for hp in range(N_PAIRS)]
                                + [gm_ref[r0:r0 + n, :]], axis=1)
        mix.append(mm(mixin, wo_ref))
    g1, b1 = ln1g_ref[...] * alpha, ln1b_ref[...] * alpha
    g2, b2 = ln2g_ref[...] * alpha, ln2b_ref[...] * alpha
    cw = cw_ref[...] * _INV_SQRT2
    cb = cb_ref[...] * _INV_SQRT2
    r1 = [_layernorm_rows(alpha * x_ref[r0:r0 + n, :] + mix[s], g1, b1) for s, (r0, n) in enumerate(subs)]
    r1b = [t.astype(bf) for t in r1]
    hb = []
    for s, (r0, n) in enumerate(subs):
        a_s[8 + r0:8 + r0 + n, :] = mm(r1b[s], wa_ref)
        hb.append(mm(r1b[s], wb_ref))
    ff = []
    for s, (r0, n) in enumerate(subs):
        conv = cb + cw[0:1, :] * a_s[6 + r0:6 + r0 + n, :]
        conv = conv + cw[1:2, :] * a_s[7 + r0:7 + r0 + n, :]
        conv = conv + cw[2:3, :] * a_s[8 + r0:8 + r0 + n, :]
        ff.append(mm((conv * (1.0 + lax.erf(conv)) * hb[s]).astype(bf), wd_ref))
    a_s[0:8, :] = a_s[tm:tm + 8, :]
    r2 = [_layernorm_rows(r1[s] + ff[s], g2, b2) for s in range(len(subs))]
    gate = [jax.nn.sigmoid(mm(r2[s].astype(bf), wg_ref) + bg_ref[...]) for s in range(len(subs))]
    for s, (r0, n) in enumerate(subs):
        o_ref[r0:r0 + n, :] = _layernorm_rows(r2[s] + gate[s] * pe[s], ln3g_ref[...], ln3b_ref[...])


def _const_spec(shape):
    nd = len(shape)
    return pl.BlockSpec(shape, lambda *_: (0,) * nd, pipeline_mode=pl.Buffered(1))


def _layer(x, p, positions, w_in, ln_z_g, ln_z_b, w_s, b_s, w_o, ln1_g, ln1_b, w_ff_a, w_ff_b, conv_w, conv_b,
           w_ff_down, ln2_g, ln2_b, w_ple_gate, b_ple_gate, w_ple_in, ln3_g, ln3_b, *, alpha, tm1, ts1, tm3, ts3):
    B, S, D = x.shape
    N = B * S
    F = w_ff_a.shape[1]
    bf = jnp.bfloat16
    xf = x.reshape(N, D)

    inv = np.float32(ROPE_THETA ** (-np.arange(0, ROPE_DIM, 2, dtype=np.float64) / ROPE_DIM)).reshape(8, 1)
    pos3 = positions.reshape(N // tm1, 1, tm1)
    tps1 = S // tm1
    n_l = S // N_RES
    row = lambda v: v.reshape(1, -1)
    later_w = [w_o, w_ff_a, w_ff_b, w_ff_down, w_ple_gate, w_ple_in]
    later_scale = (1.0, 1.0 / alpha, 1.0 / alpha, 0.5 / _INV_SQRT2, 1.0 / alpha, 1.0)
    slab = lambda w: pl.BlockSpec((w.shape[0] // CAST_STEPS, w.shape[1]),
                                  lambda i: (jnp.minimum(i, CAST_STEPS - 1), 0))
    q, k, v, gm, w_o_b, w_a_b, w_b_b, w_d_b, w_g_b, w_p_b = pl.pallas_call(
        functools.partial(_proj_kernel, tm=tm1, ts=ts1, cast_scales=later_scale),
        grid=(N // tm1,),
        in_specs=[
            pl.BlockSpec((tm1, D), lambda i: (i, 0)),
            pl.BlockSpec((None, 1, tm1), lambda i: (i, 0, 0)),
            _const_spec((8, 1)),
            _const_spec(w_in.shape),
            _const_spec((D_GMLP, 1)),
            _const_spec((D_GMLP, 1)),
            _const_spec((N_GROUPS, CHUNK, CHUNK)),
            _const_spec((N_GROUPS, CHUNK)),
        ] + [slab(w) for w in later_w],
        out_specs=[pl.BlockSpec((N_PAIRS, None, N_RES, tm1 // N_RES, LANES),
                                lambda i: (0, i // tps1, 0, i % tps1, 0))] * 3
        + [pl.BlockSpec((tm1, D_GMLP), lambda i: (i, 0))] + [slab(w) for w in later_w],
        out_shape=[jax.ShapeDtypeStruct((N_PAIRS, B, N_RES, n_l, LANES), jnp.float32)] * 3
        + [jax.ShapeDtypeStruct((N, D_GMLP), bf)] + [jax.ShapeDtypeStruct(w.shape, bf) for w in later_w],
        scratch_shapes=[pltpu.VMEM((D, 3 * D_ATTN), bf), pltpu.VMEM((2 * D_GMLP, D), bf)]
        + [pltpu.VMEM((3 * N_PAIRS, tm1, LANES), jnp.float32)] * 2,
        compiler_params=pltpu.CompilerParams(dimension_semantics=("arbitrary",), vmem_limit_bytes=VMEM_LIMIT),
        name="proj",
    )(xf, pos3, jnp.asarray(inv), w_in, ln_z_g.reshape(-1, 1), ln_z_b.reshape(-1, 1), w_s, b_s, *later_w)

    masks = jnp.asarray(_branch_masks())
    seq_spec = pl.BlockSpec((None, None, N_RES, n_l, LANES), lambda b, hp: (hp, b, 0, 0, 0))
    state = pltpu.VMEM((N_RES, WINDOW_BLOCK, LANES), jnp.float32)
    attn = pl.pallas_call(
        _attn_kernel,
        grid=(B, N_PAIRS),
        in_specs=[seq_spec, seq_spec, seq_spec, _const_spec(masks.shape)],
        out_specs=seq_spec,
        out_shape=jax.ShapeDtypeStruct((N_PAIRS, B, N_RES, n_l, LANES), bf),
        scratch_shapes=[state] * 3,
        compiler_params=pltpu.CompilerParams(dimension_semantics=("arbitrary", "arbitrary"),
                                             vmem_limit_bytes=VMEM_LIMIT),
        name="attn",
    )(q, k, v, masks)

    Dp = p.shape[-1]
    tps3 = S // tm3
    out = pl.pallas_call(
        functools.partial(_tail_kernel, tm=tm3, ts=ts3, tiles_per_seq=tps3, alpha=alpha),
        grid=(N // tm3,),
        in_specs=[
            pl.BlockSpec((N_PAIRS, None, N_RES, tm3 // N_RES, LANES), lambda i: (0, i // tps3, 0, i % tps3, 0)),
            pl.BlockSpec((tm3, D_GMLP), lambda i: (i, 0)),
            pl.BlockSpec((tm3, D), lambda i: (i, 0)),
            pl.BlockSpec((tm3, Dp), lambda i: (i, 0)),
            _const_spec((D, D)), _const_spec((1, D)), _const_spec((1, D)),
            _const_spec((D, F)), _const_spec((D, F)), _const_spec((3, F)), _const_spec((1, F)),
            _const_spec((F, D)), _const_spec((1, D)), _const_spec((1, D)),
            _const_spec((D, D)), _const_spec((1, D)), _const_spec((Dp, D)), _const_spec((1, D)), _const_spec((1, D)),
        ],
        out_specs=pl.BlockSpec((tm3, D), lambda i: (i, 0)),
        out_shape=jax.ShapeDtypeStruct((N, D), jnp.float32),
        scratch_shapes=[pltpu.VMEM((tm3 + 8, F), jnp.float32), pltpu.VMEM((N_PAIRS, tm3, LANES), jnp.float32)],
        compiler_params=pltpu.CompilerParams(dimension_semantics=("arbitrary",), vmem_limit_bytes=VMEM_LIMIT),
        name="tail",
    )(attn, gm, xf, p.reshape(N, Dp), w_o_b, row(ln1_g), row(ln1_b),
      w_a_b, w_b_b, conv_w, row(conv_b), w_d_b, row(ln2_g), row(ln2_b),
      w_g_b, row(b_ple_gate), w_p_b, row(ln3_g), row(ln3_b))
    return out.reshape(B, S, D)


def kernel(x, p, positions, w_in, ln_z_g, ln_z_b, w_s, b_s, w_o, ln1_g, ln1_b, w_ff_a, w_ff_b, conv_w, conv_b,
           w_ff_down, ln2_g, ln2_b, w_ple_gate, b_ple_gate, w_ple_in, ln3_g, ln3_b):
    depth = w_in.shape[0]
    alpha = (2.0 * depth) ** 0.25
    for i in range(depth):
        x = _layer(x, p[i], positions, w_in[i], ln_z_g[i], ln_z_b[i], w_s[i], b_s[i], w_o[i], ln1_g[i], ln1_b[i],
                   w_ff_a[i], w_ff_b[i], conv_w[i], conv_b[i], w_ff_down[i], ln2_g[i], ln2_b[i],
                   w_ple_gate[i], b_ple_gate[i], w_ple_in[i], ln3_g[i], ln3_b[i],
                   alpha=alpha, tm1=512, ts1=256, tm3=512, ts3=256)
    return x
```

```python
import functools
import math

import numpy as np
import jax
import jax.numpy as jnp
from jax import lax
from jax.experimental import pallas as pl
from jax.experimental.pallas import tpu as pltpu

HEAD_DIM = 64
N_HEADS = 8
D_ATTN = 512
D_GMLP = 512
N_GROUPS = 8
CHUNK = 128
ROPE_THETA = 500000.0
ROPE_DIM = 16
LN_EPS = 1e-5
NEG_INF = -1e30
WINDOW_BLOCK = 128
LANES = 128
N_PAIRS = D_ATTN // LANES
N_RES = 16
BLOCKS_PER_STEP = 16
CAST_STEPS = 16
LOOKAHEAD = 1
VMEM_LIMIT = 56 * 1024 * 1024

_INV_SQRT2 = 0.7071067811865476


def _gelu(t):
    return 0.5 * t * (1.0 + lax.erf(t * _INV_SQRT2))


def _layernorm_rows(t, g, b):
    mu = jnp.mean(t, axis=-1, keepdims=True)
    d = t - mu
    var = jnp.mean(d * d, axis=-1, keepdims=True)
    return d * lax.rsqrt(var + LN_EPS) * g + b


def _proj_kernel(x_ref, pos_ref, inv_ref, win_ref, lnzg_ref, lnzb_ref, ws_ref, bs_ref, *rest, tm, ts, cast_scales):
    n_cast = len(cast_scales)
    cast_in, rest = rest[:n_cast], rest[n_cast:]
    q_ref, k_ref, v_ref, gm_ref = rest[:4]
    cast_out = rest[4:4 + n_cast]
    wqkv_ref, wuzT_ref, nat_s, mid_s = rest[4 + n_cast:]
    f32, bf = jnp.float32, jnp.bfloat16
    i = pl.program_id(0)

    @pl.when(i == 0)
    def _():
        wqkv_ref[...] = win_ref[:, :3 * D_ATTN].astype(bf)
        wuzT_ref[...] = win_ref[:, 3 * D_ATTN:].T.astype(bf)

    @pl.when(i < CAST_STEPS)
    def _():
        for src, dst, c in zip(cast_in, cast_out, cast_scales):
            dst[...] = (src[...] if c == 1.0 else src[...] * c).astype(bf)

    subs = [(s * ts, ts) for s in range(tm // ts)]
    xb = [x_ref[r0:r0 + n, :].astype(bf) for (r0, n) in subs]
    h_t = [lax.dot_general(wuzT_ref[...], xb[s], (((1,), (1,)), ((), ())), preferred_element_type=f32)
           for s in range(len(subs))]

    pos = pos_ref[...].astype(f32)
    ang = pos * inv_ref[...]
    cos_t = jnp.cos(ang)
    sin_t = jnp.sin(ang)
    ones48 = jnp.ones((HEAD_DIM - ROPE_DIM, tm), f32)
    zeros48 = jnp.zeros((HEAD_DIM - ROPE_DIM, tm), f32)
    c_tab = jnp.concatenate([cos_t, cos_t, ones48, cos_t, cos_t, ones48], axis=0).T
    s_tab = jnp.concatenate([-sin_t, sin_t, zeros48, -sin_t, sin_t, zeros48], axis=0).T
    lane = lax.broadcasted_iota(jnp.int32, (1, LANES), 1)
    first_half = (lane % HEAD_DIM) < (ROPE_DIM // 2)

    def rope(t, t0, n):
        up = pltpu.roll(t, LANES - ROPE_DIM // 2, 1)
        dn = pltpu.roll(t, ROPE_DIM // 2, 1)
        return t * c_tab[t0:t0 + n] + jnp.where(first_half, up, dn) * s_tab[t0:t0 + n]

    scale = math.log2(math.e) / math.sqrt(HEAD_DIM)

    def qkv_slabs(s, t0, n):
        for j in range(3 * D_ATTN // (2 * LANES)):
            hj = jnp.dot(xb[s], wqkv_ref[:, j * 2 * LANES:(j + 1) * 2 * LANES], preferred_element_type=f32)
            for half in range(2):
                idx = 2 * j + half
                t = hj[:, half * LANES:(half + 1) * LANES]
                if idx < N_PAIRS:
                    t = rope(t, t0, n) * scale
                elif idx < 2 * N_PAIRS:
                    t = rope(t, t0, n)
                nat_s[idx, t0:t0 + n, :] = t

    row = lax.broadcasted_iota(jnp.int32, (CHUNK, CHUNK), 0)
    col = lax.broadcasted_iota(jnp.int32, (CHUNK, CHUNK), 1)
    keep = col <= row
    w_sp = [jnp.where(keep, ws_ref[g], 0.0).astype(bf) for g in range(N_GROUPS)]
    n_chunks = ts // CHUNK
    for s, (t0, n) in enumerate(subs):
        qkv_slabs(s, t0, n)
        u_t = _gelu(h_t[s][:D_GMLP])
        z_t = _gelu(h_t[s][D_GMLP:])
        mu = jnp.mean(z_t, axis=0, keepdims=True)
        d = z_t - mu
        var = jnp.mean(d * d, axis=0, keepdims=True)
        zn_t = (d * lax.rsqrt(var + LN_EPS) * lnzg_ref[...] + lnzb_ref[...]).astype(bf)
        mixed = []
        for g in range(N_GROUPS):
            r0 = g * HEAD_DIM
            lhs = jnp.concatenate([zn_t[r0:r0 + HEAD_DIM, c * CHUNK:(c + 1) * CHUNK] for c in range(n_chunks)],
                                  axis=0)
            res = lax.dot_general(lhs, w_sp[g], (((1,), (1,)), ((), ())),
                                  preferred_element_type=f32) + bs_ref[g:g + 1, :]
            mixed.append(jnp.concatenate([res[c * HEAD_DIM:(c + 1) * HEAD_DIM] for c in range(n_chunks)],
                                         axis=1))
        mixed_t = jnp.concatenate(mixed, axis=0)
        gm_ref[t0:t0 + n, :] = (u_t * mixed_t).T.astype(bf)

    quarter = tm // 4
    n_l = tm // N_RES
    for idx in range(3 * N_PAIRS):
        for r in range(4):
            mid_s[idx, r * quarter:(r + 1) * quarter, :] = nat_s[idx, pl.ds(r, quarter, stride=4), :]
    for which, o_ref in enumerate((q_ref, k_ref, v_ref)):
        for hp in range(N_PAIRS):
            idx = which * N_PAIRS + hp
            for r_lo in range(4):
                for r_hi in range(4):
                    o_ref[hp, 4 * r_hi + r_lo] = mid_s[idx, pl.ds(r_lo * quarter + r_hi, n_l, stride=4), :]


def _branch_masks():
    w = WINDOW_BLOCK
    out = np.zeros((6, w, 2 * w), np.float32)
    rq, i8 = np.divmod(np.arange(w), 8)
    rk, j16 = np.divmod(np.arange(2 * w), 16)
    base = 16 * (i8[:, None] - j16[None, :]) + (rq[:, None] - rk[None, :])
    aq, i32 = np.divmod(np.arange(w), 32)
    ak, j64 = np.divmod(np.arange(2 * w), 64)
    base4 = 4 * (i32[:, None] - j64[None, :]) + (aq[:, None] - ak[None, :])
    base16 = np.arange(w)[:, None] - np.arange(2 * w)[None, :]
    for c, b in enumerate((base, base4, base16)):
        dist = b + w
        out[2 * c] = (dist >= 0) & (dist <= w)
        dist0 = b
        out[2 * c + 1] = (dist0 >= 0) & (dist0 <= w)
    return out


def _attn_kernel(q_s, k_s, v_s, mask_ref, o_ref, acc_s, m_s, l_s):
    n_l = q_s.shape[1]
    n_spans = n_l // WINDOW_BLOCK

    lane = lax.broadcasted_iota(jnp.int32, (1, LANES), 1)
    head0 = lane < HEAD_DIM

    def gather(ref, chunks):
        return jnp.concatenate([ref[r, pl.ds(st, n), :] for (r, st, n) in chunks], axis=0)

    def for_blocks(chains, first, last):
        w = WINDOW_BLOCK
        m_olds = None if first else [gather(m_s, ch[2]) for ch in chains]

        def scores(c):
            q_chunks, kv_chunks, _, _ = chains[c]
            qv = gather(q_s, q_chunks)
            kb = gather(k_s, kv_chunks).astype(jnp.bfloat16)
            q2 = jnp.concatenate([jnp.where(head0, qv, 0.0), jnp.where(head0, 0.0, qv)],
                                 axis=0).astype(jnp.bfloat16)
            return lax.dot_general(q2, kb, (((1,), (1,)), ((), ())),
                                   preferred_element_type=jnp.float32)

        def finish(c, s2):
            q_chunks, kv_chunks, st_chunks, mask = chains[c]
            valid = mask > 0.5
            m_old = None if first else m_olds[c]
            ps, ms = [], []
            for hh in range(2):
                s = jnp.where(valid, s2[hh * w:(hh + 1) * w], NEG_INF)
                t = jnp.maximum(s[:, :LANES], s[:, LANES:])
                if not first:
                    own = head0 if hh == 0 else jnp.logical_not(head0)
                    t = jnp.maximum(t, jnp.where(own, m_old, NEG_INF))
                m_new = jnp.max(t, axis=1, keepdims=True)
                ps.append(jnp.exp2(s - m_new))
                ms.append(m_new)
            p2 = jnp.concatenate(ps, axis=0).astype(jnp.bfloat16)
            m_b = jnp.where(head0, ms[0], ms[1])
            vb = gather(v_s, kv_chunks).astype(jnp.bfloat16)
            vb1 = jnp.concatenate([vb, jnp.ones_like(vb)], axis=1)
            o2 = jnp.dot(p2, vb1, preferred_element_type=jnp.float32)
            o_pair = jnp.where(head0, o2[:w, :LANES], o2[w:, :LANES])
            l_b = jnp.where(head0, o2[:w, LANES:], o2[w:, LANES:])
            if first:
                acc, l_tot = o_pair, l_b
            else:
                alpha = jnp.exp2(m_old - m_b)
                acc = gather(acc_s, st_chunks) * alpha + o_pair
                l_tot = gather(l_s, st_chunks) * alpha + l_b
            if last:
                (r, st, n), = q_chunks
                o_ref[r, pl.ds(st, n), :] = (acc / l_tot).astype(o_ref.dtype)
            else:
                off = 0
                for (r, st, n) in st_chunks:
                    acc_s[r, pl.ds(st, n), :] = acc[off:off + n]
                    m_s[r, pl.ds(st, n), :] = m_b[off:off + n]
                    l_s[r, pl.ds(st, n), :] = l_tot[off:off + n]
                    off += n

        pending = {c: scores(c) for c in range(min(LOOKAHEAD, len(chains)))}
        for c in range(len(chains)):
            s2 = pending.pop(c) if c in pending else scores(c)
            if LOOKAHEAD and c + LOOKAHEAD < len(chains):
                pending[c + LOOKAHEAD] = scores(c + LOOKAHEAD)
            finish(c, s2)

    for span in range(n_spans):
        def d1_block(n, span=span):
            nb = span * 16 + n
            q0 = pl.multiple_of(nb * 8, 8)
            k0 = pl.multiple_of(jnp.maximum(nb - 1, 0) * 8, 8)
            s0 = pl.multiple_of(n * 8, 8)
            return ([(r, q0, 8) for r in range(N_RES)], [(r, k0, 16) for r in range(N_RES)],
                    [(r, s0, 8) for r in range(N_RES)], mask_ref[jnp.where(nb == 0, 1, 0)])

        def d1_body(it, carry, d1_block=d1_block):
            for_blocks([d1_block(BLOCKS_PER_STEP * it + j) for j in range(BLOCKS_PER_STEP)], True, False)
            return carry
        lax.fori_loop(0, 16 // BLOCKS_PER_STEP, d1_body, 0)

        def d4_block(idx, span=span):
            r4 = idx // 4
            n = span * 4 + idx % 4
            q0 = pl.multiple_of(n * 32, 32)
            k0 = pl.multiple_of(jnp.maximum(n - 1, 0) * 32, 32)
            s0 = pl.multiple_of((idx % 4) * 32, 32)
            return ([(r4 + 4 * a, q0, 32) for a in range(4)], [(r4 + 4 * a, k0, 64) for a in range(4)],
                    [(r4 + 4 * a, s0, 32) for a in range(4)], mask_ref[2 + jnp.where(n == 0, 1, 0)])

        def d4_body(it, carry, d4_block=d4_block):
            for_blocks([d4_block(BLOCKS_PER_STEP * it + j) for j in range(BLOCKS_PER_STEP)], False, False)
            return carry
        lax.fori_loop(0, 16 // BLOCKS_PER_STEP, d4_body, 0)

        def d16_block(r16, span=span):
            q0 = span * WINDOW_BLOCK
            k0 = max(span - 1, 0) * WINDOW_BLOCK
            return ([(r16, q0, WINDOW_BLOCK)], [(r16, k0, 2 * WINDOW_BLOCK)],
                    [(r16, 0, WINDOW_BLOCK)], mask_ref[4 + (1 if span == 0 else 0)])

        def d16_body(it, carry, d16_block=d16_block):
            for_blocks([d16_block(BLOCKS_PER_STEP * it + j) for j in range(BLOCKS_PER_STEP)], False, True)
            return carry
        lax.fori_loop(0, N_RES // BLOCKS_PER_STEP, d16_body, 0)


def _tail_kernel(attn_ref, gm_ref, x_ref, p_ref, wo_ref, ln1g_ref, ln1b_ref, wa_ref, wb_ref, cw_ref, cb_ref,
                 wd_ref, ln2g_ref, ln2b_ref, wg_ref, bg_ref, wp_ref, ln3g_ref, ln3b_ref, o_ref, a_s, un_s,
                 *, tm, ts, tiles_per_seq, alpha):
    i = pl.program_id(0)

    @pl.when(i % tiles_per_seq == 0)
    def _():
        a_s[0:8, :] = jnp.zeros((8, a_s.shape[1]), jnp.float32)

    subs = [(s * ts, ts) for s in range(tm // ts)]
    f32, bf = jnp.float32, jnp.bfloat16
    mm = lambda a, w_ref: jnp.dot(a, w_ref[...], preferred_element_type=f32)

    pe = [mm(p_ref[r0:r0 + n, :].astype(bf), wp_ref) for (r0, n) in subs]

    n_l = tm // N_RES
    for hp in range(N_PAIRS):
        for r in range(N_RES):
            un_s[hp, pl.ds(r, n_l, stride=N_RES), :] = attn_ref[hp, r].astype(f32)

    mix = []
    for (r0, n) in subs:
        mixin = jnp.concatenate([un_s[hp, r0:r0 + n, :].astype(bf) for hp in range(N_PAIRS)]
                                + [gm_ref[r0:r0 + n, :]], axis=1)
        mix.append(mm(mixin, wo_ref))
    g1, b1 = ln1g_ref[...] * alpha, ln1b_ref[...] * alpha
    g2, b2 = ln2g_ref[...] * alpha, ln2b_ref[...] * alpha
    cw = cw_ref[...] * _INV_SQRT2
    cb = cb_ref[...] * _INV_SQRT2
    r1 = [_layernorm_rows(alpha * x_ref[r0:r0 + n, :] + mix[s], g1, b1) for s, (r0, n) in enumerate(subs)]
    r1b = [t.astype(bf) for t in r1]
    hb = []
    for s, (r0, n) in enumerate(subs):
        a_s[8 + r0:8 + r0 + n, :] = mm(r1b[s], wa_ref)
        hb.append(mm(r1b[s], wb_ref))
    ff = []
    for s, (r0, n) in enumerate(subs):
        conv = cb + cw[0:1, :] * a_s[6 + r0:6 + r0 + n, :]
        conv = conv + cw[1:2, :] * a_s[7 + r0:7 + r0 + n, :]
        conv = conv + cw[2:3, :] * a_s[8 + r0:8 + r0 + n, :]
        ff.append(mm((conv * (1.0 + lax.erf(conv)) * hb[s]).astype(bf), wd_ref))
    a_s[0:8, :] = a_s[tm:tm + 8, :]
    r2 = [_layernorm_rows(r1[s] + ff[s], g2, b2) for s in range(len(subs))]
    gate = [jax.nn.sigmoid(mm(r2[s].astype(bf), wg_ref) + bg_ref[...]) for s in range(len(subs))]
    for s, (r0, n) in enumerate(subs):
        o_ref[r0:r0 + n, :] = _layernorm_rows(r2[s] + gate[s] * pe[s], ln3g_ref[...], ln3b_ref[...])


def _const_spec(shape):
    nd = len(shape)
    return pl.BlockSpec(shape, lambda *_: (0,) * nd, pipeline_mode=pl.Buffered(1))


def _layer(x, p, positions, w_in, ln_z_g, ln_z_b, w_s, b_s, w_o, ln1_g, ln1_b, w_ff_a, w_ff_b, conv_w, conv_b,
           w_ff_down, ln2_g, ln2_b, w_ple_gate, b_ple_gate, w_ple_in, ln3_g, ln3_b, *, alpha, tm1, ts1, tm3, ts3):
    B, S, D = x.shape
    N = B * S
    F = w_ff_a.shape[1]
    bf = jnp.bfloat16
    xf = x.reshape(N, D)

    inv = np.float32(ROPE_THETA ** (-np.arange(0, ROPE_DIM, 2, dtype=np.float64) / ROPE_DIM)).reshape(8, 1)
    pos3 = positions.reshape(N // tm1, 1, tm1)
    tps1 = S // tm1
    n_l = S // N_RES
    row = lambda v: v.reshape(1, -1)
    later_w = [w_o, w_ff_a, w_ff_b, w_ff_down, w_ple_gate, w_ple_in]
    later_scale = (1.0, 1.0 / alpha, 1.0 / alpha, 0.5 / _INV_SQRT2, 1.0 / alpha, 1.0)
    slab = lambda w: pl.BlockSpec((w.shape[0] // CAST_STEPS, w.shape[1]),
                                  lambda i: (jnp.minimum(i, CAST_STEPS - 1), 0))
    q, k, v, gm, w_o_b, w_a_b, w_b_b, w_d_b, w_g_b, w_p_b = pl.pallas_call(
        functools.partial(_proj_kernel, tm=tm1, ts=ts1, cast_scales=later_scale),
        grid=(N // tm1,),
        in_specs=[
            pl.BlockSpec((tm1, D), lambda i: (i, 0)),
            pl.BlockSpec((None, 1, tm1), lambda i: (i, 0, 0)),
            _const_spec((8, 1)),
            _const_spec(w_in.shape),
            _const_spec((D_GMLP, 1)),
            _const_spec((D_GMLP, 1)),
            _const_spec((N_GROUPS, CHUNK, CHUNK)),
            _const_spec((N_GROUPS, CHUNK)),
        ] + [slab(w) for w in later_w],
        out_specs=[pl.BlockSpec((N_PAIRS, None, N_RES, tm1 // N_RES, LANES),
                                lambda i: (0, i // tps1, 0, i % tps1, 0))] * 3
        + [pl.BlockSpec((tm1, D_GMLP), lambda i: (i, 0))] + [slab(w) for w in later_w],
        out_shape=[jax.ShapeDtypeStruct((N_PAIRS, B, N_RES, n_l, LANES), jnp.float32)] * 3
        + [jax.ShapeDtypeStruct((N, D_GMLP), bf)] + [jax.ShapeDtypeStruct(w.shape, bf) for w in later_w],
        scratch_shapes=[pltpu.VMEM((D, 3 * D_ATTN), bf), pltpu.VMEM((2 * D_GMLP, D), bf)]
        + [pltpu.VMEM((3 * N_PAIRS, tm1, LANES), jnp.float32)] * 2,
        compiler_params=pltpu.CompilerParams(dimension_semantics=("arbitrary",), vmem_limit_bytes=VMEM_LIMIT),
        name="proj",
    )(xf, pos3, jnp.asarray(inv), w_in, ln_z_g.reshape(-1, 1), ln_z_b.reshape(-1, 1), w_s, b_s, *later_w)

    masks = jnp.asarray(_branch_masks())
    seq_spec = pl.BlockSpec((None, None, N_RES, n_l, LANES), lambda b, hp: (hp, b, 0, 0, 0))
    state = pltpu.VMEM((N_RES, WINDOW_BLOCK, LANES), jnp.float32)
    attn = pl.pallas_call(
        _attn_kernel,
        grid=(B, N_PAIRS),
        in_specs=[seq_spec, seq_spec, seq_spec, _const_spec(masks.shape)],
        out_specs=seq_spec,
        out_shape=jax.ShapeDtypeStruct((N_PAIRS, B, N_RES, n_l, LANES), bf),
        scratch_shapes=[state] * 3,
        compiler_params=pltpu.CompilerParams(dimension_semantics=("arbitrary", "arbitrary"),
                                             vmem_limit_bytes=VMEM_LIMIT),
        name="attn",
    )(q, k, v, masks)

    Dp = p.shape[-1]
    tps3 = S // tm3
    out = pl.pallas_call(
        functools.partial(_tail_kernel, tm=tm3, ts=ts3, tiles_per_seq=tps3, alpha=alpha),
        grid=(N // tm3,),
        in_specs=[
            pl.BlockSpec((N_PAIRS, None, N_RES, tm3 // N_RES, LANES), lambda i: (0, i // tps3, 0, i % tps3, 0)),
            pl.BlockSpec((tm3, D_GMLP), lambda i: (i, 0)),
            pl.BlockSpec((tm3, D), lambda i: (i, 0)),
            pl.BlockSpec((tm3, Dp), lambda i: (i, 0)),
            _const_spec((D, D)), _const_spec((1, D)), _const_spec((1, D)),
            _const_spec((D, F)), _const_spec((D, F)), _const_spec((3, F)), _const_spec((1, F)),
            _const_spec((F, D)), _const_spec((1, D)), _const_spec((1, D)),
            _const_spec((D, D)), _const_spec((1, D)), _const_spec((Dp, D)), _const_spec((1, D)), _const_spec((1, D)),
        ],
        out_specs=pl.BlockSpec((tm3, D), lambda i: (i, 0)),
        out_shape=jax.ShapeDtypeStruct((N, D), jnp.float32),
        scratch_shapes=[pltpu.VMEM((tm3 + 8, F), jnp.float32), pltpu.VMEM((N_PAIRS, tm3, LANES), jnp.float32)],
        compiler_params=pltpu.CompilerParams(dimension_semantics=("arbitrary",), vmem_limit_bytes=VMEM_LIMIT),
        name="tail",
    )(attn, gm, xf, p.reshape(N, Dp), w_o_b, row(ln1_g), row(ln1_b),
      w_a_b, w_b_b, conv_w, row(conv_b), w_d_b, row(ln2_g), row(ln2_b),
      w_g_b, row(b_ple_gate), w_p_b, row(ln3_g), row(ln3_b))
    return out.reshape(B, S, D)


def kernel(x, p, positions, w_in, ln_z_g, ln_z_b, w_s, b_s, w_o, ln1_g, ln1_b, w_ff_a, w_ff_b, conv_w, conv_b,
           w_ff_down, ln2_g, ln2_b, w_ple_gate, b_ple_gate, w_ple_in, ln3_g, ln3_b):
    depth = w_in.shape[0]
    alpha = (2.0 * depth) ** 0.25
    for i in range(depth):
        x = _layer(x, p[i], positions, w_in[i], ln_z_g[i], ln_z_b[i], w_s[i], b_s[i], w_o[i], ln1_g[i], ln1_b[i],
                   w_ff_a[i], w_ff_b[i], conv_w[i], conv_b[i], w_ff_down[i], ln2_g[i], ln2_b[i],
                   w_ple_gate[i], b_ple_gate[i], w_ple_in[i], ln3_g[i], ln3_b[i],
                   alpha=alpha, tm1=512, ts1=256, tm3=512, ts3=256)
    return x
```

```python
import functools
import math

import numpy as np
import jax
import jax.numpy as jnp
from jax import lax
from jax.experimental import pallas as pl
from jax.experimental.pallas import tpu as pltpu

HEAD_DIM = 64
N_HEADS = 8
D_ATTN = 512
D_GMLP = 512
N_GROUPS = 8
CHUNK = 128
ROPE_THETA = 500000.0
ROPE_DIM = 16
LN_EPS = 1e-5
NEG_INF = -1e30
WINDOW_BLOCK = 128
LANES = 128
N_PAIRS = D_ATTN // LANES
N_RES = 16
CAST_STEPS = 16
LOOKAHEAD = 1
VMEM_LIMIT = 56 * 1024 * 1024

_INV_SQRT2 = 0.7071067811865476


def _gelu(t):
    return 0.5 * t * (1.0 + lax.erf(t * _INV_SQRT2))


def _layernorm_rows(t, g, b):
    mu = jnp.mean(t, axis=-1, keepdims=True)
    d = t - mu
    var = jnp.mean(d * d, axis=-1, keepdims=True)
    return d * lax.rsqrt(var + LN_EPS) * g + b


def _proj_kernel(x_ref, pos_ref, inv_ref, win_ref, lnzg_ref, lnzb_ref, ws_ref, bs_ref, *rest, tm, ts, cast_scales):
    n_cast = len(cast_scales)
    cast_in, rest = rest[:n_cast], rest[n_cast:]
    q_ref, k_ref, v_ref, gm_ref = rest[:4]
    cast_out = rest[4:4 + n_cast]
    wqkv_ref, wuzT_ref, nat_s, mid_s = rest[4 + n_cast:]
    f32, bf = jnp.float32, jnp.bfloat16
    i = pl.program_id(0)

    @pl.when(i == 0)
    def _():
        wqkv_ref[...] = win_ref[:, :3 * D_ATTN].astype(bf)
        wuzT_ref[...] = win_ref[:, 3 * D_ATTN:].T.astype(bf)

    @pl.when(i < CAST_STEPS)
    def _():
        for src, dst, c in zip(cast_in, cast_out, cast_scales):
            dst[...] = (src[...] if c == 1.0 else src[...] * c).astype(bf)

    subs = [(s * ts, ts) for s in range(tm // ts)]
    xb = [x_ref[r0:r0 + n, :].astype(bf) for (r0, n) in subs]
    h_t = [lax.dot_general(wuzT_ref[...], xb[s], (((1,), (1,)), ((), ())), preferred_element_type=f32)
           for s in range(len(subs))]

    pos = pos_ref[...].astype(f32)
    ang = pos * inv_ref[...]
    cos_t = jnp.cos(ang)
    sin_t = jnp.sin(ang)
    ones48 = jnp.ones((HEAD_DIM - ROPE_DIM, tm), f32)
    zeros48 = jnp.zeros((HEAD_DIM - ROPE_DIM, tm), f32)
    c_tab = jnp.concatenate([cos_t, cos_t, ones48, cos_t, cos_t, ones48], axis=0).T
    s_tab = jnp.concatenate([-sin_t, sin_t, zeros48, -sin_t, sin_t, zeros48], axis=0).T
    lane = lax.broadcasted_iota(jnp.int32, (1, LANES), 1)
    first_half = (lane % HEAD_DIM) < (ROPE_DIM // 2)

    def rope(t, t0, n):
        up = pltpu.roll(t, LANES - ROPE_DIM // 2, 1)
        dn = pltpu.roll(t, ROPE_DIM // 2, 1)
        return t * c_tab[t0:t0 + n] + jnp.where(first_half, up, dn) * s_tab[t0:t0 + n]

    scale = math.log2(math.e) / math.sqrt(HEAD_DIM)

    def qkv_slabs(s, t0, n):
        for j in range(3 * D_ATTN // (2 * LANES)):
            hj = jnp.dot(xb[s], wqkv_ref[:, j * 2 * LANES:(j + 1) * 2 * LANES], preferred_element_type=f32)
            for half in range(2):
                idx = 2 * j + half
                t = hj[:, half * LANES:(half + 1) * LANES]
                if idx < N_PAIRS:
                    t = rope(t, t0, n) * scale
                elif idx < 2 * N_PAIRS:
                    t = rope(t, t0, n)
                nat_s[idx, t0:t0 + n, :] = t

    row = lax.broadcasted_iota(jnp.int32, (CHUNK, CHUNK), 0)
    col = lax.broadcasted_iota(jnp.int32, (CHUNK, CHUNK), 1)
    keep = col <= row
    w_sp = [jnp.where(keep, ws_ref[g], 0.0).astype(bf) for g in range(N_GROUPS)]
    n_chunks = ts // CHUNK
    for s, (t0, n) in enumerate(subs):
        qkv_slabs(s, t0, n)
        u_t = _gelu(h_t[s][:D_GMLP])
        z_t = _gelu(h_t[s][D_GMLP:])
        mu = jnp.mean(z_t, axis=0, keepdims=True)
        d = z_t - mu
        var = jnp.mean(d * d, axis=0, keepdims=True)
        zn_t = (d * lax.rsqrt(var + LN_EPS) * lnzg_ref[...] + lnzb_ref[...]).astype(bf)
        mixed = []
        for g in range(N_GROUPS):
            r0 = g * HEAD_DIM
            lhs = jnp.concatenate([zn_t[r0:r0 + HEAD_DIM, c * CHUNK:(c + 1) * CHUNK] for c in range(n_chunks)],
                                  axis=0)
            res = lax.dot_general(lhs, w_sp[g], (((1,), (1,)), ((), ())),
                                  preferred_element_type=f32) + bs_ref[g:g + 1, :]
            mixed.append(jnp.concatenate([res[c * HEAD_DIM:(c + 1) * HEAD_DIM] for c in range(n_chunks)],
                                         axis=1))
        mixed_t = jnp.concatenate(mixed, axis=0)
        gm_ref[t0:t0 + n, :] = (u_t * mixed_t).T.astype(bf)

    quarter = tm // 4
    n_l = tm // N_RES
    for idx in range(3 * N_PAIRS):
        for r in range(4):
            mid_s[idx, r * quarter:(r + 1) * quarter, :] = nat_s[idx, pl.ds(r, quarter, stride=4), :]
    for which, o_ref in enumerate((q_ref, k_ref, v_ref)):
        for hp in range(N_PAIRS):
            idx = which * N_PAIRS + hp
            for r_lo in range(4):
                for r_hi in range(4):
                    o_ref[hp, 4 * r_hi + r_lo] = mid_s[idx, pl.ds(r_lo * quarter + r_hi, n_l, stride=4), :]


def _branch_masks():
    w = WINDOW_BLOCK
    out = np.zeros((6, w, 2 * w), np.float32)
    rq, i8 = np.divmod(np.arange(w), 8)
    rk, j16 = np.divmod(np.arange(2 * w), 16)
    base = 16 * (i8[:, None] - j16[None, :]) + (rq[:, None] - rk[None, :])
    aq, i32 = np.divmod(np.arange(w), 32)
    ak, j64 = np.divmod(np.arange(2 * w), 64)
    base4 = 4 * (i32[:, None] - j64[None, :]) + (aq[:, None] - ak[None, :])
    base16 = np.arange(w)[:, None] - np.arange(2 * w)[None, :]
    for c, b in enumerate((base, base4, base16)):
        dist = b + w
        out[2 * c] = (dist >= 0) & (dist <= w)
        dist0 = b
        out[2 * c + 1] = (dist0 >= 0) & (dist0 <= w)
    return out


def _attn_kernel(q_s, k_s, v_s, mask_ref, o_ref, acc_s, m_s, l_s):
    n_l = q_s.shape[1]
    n_spans = n_l // WINDOW_BLOCK

    lane = lax.broadcasted_iota(jnp.int32, (1, LANES), 1)
    head0 = lane < HEAD_DIM

    def gather(ref, chunks):
        return jnp.concatenate([ref[r, pl.ds(st, n), :] for (r, st, n) in chunks], axis=0)

    def run_chains(chains):
        w = WINDOW_BLOCK

        def scores(c):
            q_chunks, kv_chunks = chains[c][:2]
            qv = gather(q_s, q_chunks)
            kb = gather(k_s, kv_chunks).astype(jnp.bfloat16)
            q2 = jnp.concatenate([jnp.where(head0, qv, 0.0), jnp.where(head0, 0.0, qv)],
                                 axis=0).astype(jnp.bfloat16)
            return lax.dot_general(q2, kb, (((1,), (1,)), ((), ())),
                                   preferred_element_type=jnp.float32)

        def finish(c, s2):
            q_chunks, kv_chunks, st_chunks, mask_idx, first, last = chains[c]
            valid = mask_ref[mask_idx] > 0.5
            m_old = None if first else gather(m_s, st_chunks)
            ps, ms = [], []
            for hh in range(2):
                s = jnp.where(valid, s2[hh * w:(hh + 1) * w], NEG_INF)
                t = jnp.maximum(s[:, :LANES], s[:, LANES:])
                if not first:
                    own = head0 if hh == 0 else jnp.logical_not(head0)
                    t = jnp.maximum(t, jnp.where(own, m_old, NEG_INF))
                m_new = jnp.max(t, axis=1, keepdims=True)
                ps.append(jnp.exp2(s - m_new))
                ms.append(m_new)
            p2 = jnp.concatenate(ps, axis=0).astype(jnp.bfloat16)
            m_b = jnp.where(head0, ms[0], ms[1])
            vb = gather(v_s, kv_chunks).astype(jnp.bfloat16)
            vb1 = jnp.concatenate([vb, jnp.ones_like(vb)], axis=1)
            o2 = jnp.dot(p2, vb1, preferred_element_type=jnp.float32)
            o_pair = jnp.where(head0, o2[:w, :LANES], o2[w:, :LANES])
            l_b = jnp.where(head0, o2[:w, LANES:], o2[w:, LANES:])
            if first:
                acc, l_tot = o_pair, l_b
            else:
                alpha = jnp.exp2(m_old - m_b)
                acc = gather(acc_s, st_chunks) * alpha + o_pair
                l_tot = gather(l_s, st_chunks) * alpha + l_b
            if last:
                (r, st, n), = q_chunks
                o_ref[r, pl.ds(st, n), :] = (acc / l_tot).astype(o_ref.dtype)
            else:
                off = 0
                for (r, st, n) in st_chunks:
                    acc_s[r, pl.ds(st, n), :] = acc[off:off + n]
                    m_s[r, pl.ds(st, n), :] = m_b[off:off + n]
                    l_s[r, pl.ds(st, n), :] = l_tot[off:off + n]
                    off += n

        pending = {c: scores(c) for c in range(min(LOOKAHEAD, len(chains)))}
        for c in range(len(chains)):
            s2 = pending.pop(c) if c in pending else scores(c)
            if LOOKAHEAD and c + LOOKAHEAD < len(chains):
                pending[c + LOOKAHEAD] = scores(c + LOOKAHEAD)
            finish(c, s2)

    chains = []
    for span in range(n_spans):
        for n in range(16):
            nb = span * 16 + n
            q0, k0, s0 = nb * 8, max(nb - 1, 0) * 8, n * 8
            chains.append(([(r, q0, 8) for r in range(N_RES)], [(r, k0, 16) for r in range(N_RES)],
                           [(r, s0, 8) for r in range(N_RES)], 1 if nb == 0 else 0, True, False))
        for r4 in range(4):
            for j in range(4):
                n = span * 4 + j
                q0, k0, s0 = n * 32, max(n - 1, 0) * 32, j * 32
                pieces = [r4 + 4 * a for a in range(4)]
                chains.append(([(r, q0, 32) for r in pieces], [(r, k0, 64) for r in pieces],
                               [(r, s0, 32) for r in pieces], 3 if n == 0 else 2, False, False))
        for r16 in range(N_RES):
            q0, k0 = span * WINDOW_BLOCK, max(span - 1, 0) * WINDOW_BLOCK
            chains.append(([(r16, q0, WINDOW_BLOCK)], [(r16, k0, 2 * WINDOW_BLOCK)], [(r16, 0, WINDOW_BLOCK)],
                           5 if span == 0 else 4, False, True))
    run_chains(chains)


def _tail_kernel(attn_ref, gm_ref, x_ref, p_ref, wo_ref, ln1g_ref, ln1b_ref, wa_ref, wb_ref, cw_ref, cb_ref,
                 wd_ref, ln2g_ref, ln2b_ref, wg_ref, bg_ref, wp_ref, ln3g_ref, ln3b_ref, o_ref, a_s, un_s,
                 *, tm, ts, tiles_per_seq, alpha):
    i = pl.program_id(0)

    @pl.when(i % tiles_per_seq == 0)
    def _():
        a_s[0:8, :] = jnp.zeros((8, a_s.shape[1]), jnp.float32)

    subs = [(s * ts, ts) for s in range(tm // ts)]
    f32, bf = jnp.float32, jnp.bfloat16
    mm = lambda a, w_ref: jnp.dot(a, w_ref[...], preferred_element_type=f32)

    pe = [mm(p_ref[r0:r0 + n, :].astype(bf), wp_ref) for (r0, n) in subs]

    n_l = tm // N_RES
    for hp in range(N_PAIRS):
        for r in range(N_RES):
            un_s[hp, pl.ds(r, n_l, stride=N_RES), :] = attn_ref[hp, r].astype(f32)

    mix = []
    for (r0, n) in subs:
        mixin = jnp.concatenate([un_s[hp, r0:r0 + n, :].astype(bf) for hp in range(N_PAIRS)]
                                + [gm_ref[r0:r0 + n, :]], axis=1)
        mix.append(mm(mixin, wo_ref))
    g1, b1 = ln1g_ref[...] * alpha, ln1b_ref[...] * alpha
    g2, b2 = ln2g_ref[...] * alpha, ln2b_ref[...] * alpha
    cw = cw_ref[...] * _INV_SQRT2
    cb = cb_ref[...] * _INV_SQRT2
    r1 = [_layernorm_rows(alpha * x_ref[r0:r0 + n, :] + mix[s], g1, b1) for s, (r0, n) in enumerate(subs)]
    r1b = [t.astype(bf) for t in r1]
    hb = []
    for s, (r0, n) in enumerate(subs):
        a_s[8 + r0:8 + r0 + n, :] = mm(r1b[s], wa_ref)
        hb.append(mm(r1b[s], wb_ref))
    ff = []
    for s, (r0, n) in enumerate(subs):
        conv = cb + cw[0:1, :] * a_s[6 + r0:6 + r0 + n, :]
        conv = conv + cw[1:2, :] * a_s[7 + r0:7 + r0 + n, :]
        conv = conv + cw[2:3, :] * a_s[8 + r0:8 + r0 + n, :]
        ff.append(mm((conv * (1.0 + lax.erf(conv)) * hb[s]).astype(bf), wd_ref))
    a_s[0:8, :] = a_s[tm:tm + 8, :]
    r2 = [_layernorm_rows(r1[s] + ff[s], g2, b2) for s in range(len(subs))]
    gate = [jax.nn.sigmoid(mm(r2[s].astype(bf), wg_ref) + bg_ref[...]) for s in range(len(subs))]
    for s, (r0, n) in enumerate(subs):
        o_ref[r0:r0 + n, :] = _layernorm_rows(r2[s] + gate[s] * pe[s], ln3g_ref[...], ln3b_ref[...])


def _const_spec(shape):
    nd = len(shape)
    return pl.BlockSpec(shape, lambda *_: (0,) * nd, pipeline_mode=pl.Buffered(1))


def _layer(x, p, positions, w_in, ln_z_g, ln_z_b, w_s, b_s, w_o, ln1_g, ln1_b, w_ff_a, w_ff_b, conv_w, conv_b,
           w_ff_down, ln2_g, ln2_b, w_ple_gate, b_ple_gate, w_ple_in, ln3_g, ln3_b, *, alpha, tm1, ts1, tm3, ts3):
    B, S, D = x.shape
    N = B * S
    F = w_ff_a.shape[1]
    bf = jnp.bfloat16
    xf = x.reshape(N, D)

    inv = np.float32(ROPE_THETA ** (-np.arange(0, ROPE_DIM, 2, dtype=np.float64) / ROPE_DIM)).reshape(8, 1)
    pos3 = positions.reshape(N // tm1, 1, tm1)
    tps1 = S // tm1
    n_l = S // N_RES
    row = lambda v: v.reshape(1, -1)
    later_w = [w_o, w_ff_a, w_ff_b, w_ff_down, w_ple_gate, w_ple_in]
    later_scale = (1.0, 1.0 / alpha, 1.0 / alpha, 0.5 / _INV_SQRT2, 1.0 / alpha, 1.0)
    slab = lambda w: pl.BlockSpec((w.shape[0] // CAST_STEPS, w.shape[1]),
                                  lambda i: (jnp.minimum(i, CAST_STEPS - 1), 0))
    q, k, v, gm, w_o_b, w_a_b, w_b_b, w_d_b, w_g_b, w_p_b = pl.pallas_call(
        functools.partial(_proj_kernel, tm=tm1, ts=ts1, cast_scales=later_scale),
        grid=(N // tm1,),
        in_specs=[
            pl.BlockSpec((tm1, D), lambda i: (i, 0)),
            pl.BlockSpec((None, 1, tm1), lambda i: (i, 0, 0)),
            _const_spec((8, 1)),
            _const_spec(w_in.shape),
            _const_spec((D_GMLP, 1)),
            _const_spec((D_GMLP, 1)),
            _const_spec((N_GROUPS, CHUNK, CHUNK)),
            _const_spec((N_GROUPS, CHUNK)),
        ] + [slab(w) for w in later_w],
        out_specs=[pl.BlockSpec((N_PAIRS, None, N_RES, tm1 // N_RES, LANES),
                                lambda i: (0, i // tps1, 0, i % tps1, 0))] * 3
        + [pl.BlockSpec((tm1, D_GMLP), lambda i: (i, 0))] + [slab(w) for w in later_w],
        out_shape=[jax.ShapeDtypeStruct((N_PAIRS, B, N_RES, n_l, LANES), jnp.float32)] * 3
        + [jax.ShapeDtypeStruct((N, D_GMLP), bf)] + [jax.ShapeDtypeStruct(w.shape, bf) for w in later_w],
        scratch_shapes=[pltpu.VMEM((D, 3 * D_ATTN), bf), pltpu.VMEM((2 * D_GMLP, D), bf)]
        + [pltpu.VMEM((3 * N_PAIRS, tm1, LANES), jnp.float32)] * 2,
        compiler_params=pltpu.CompilerParams(dimension_semantics=("arbitrary",), vmem_limit_bytes=VMEM_LIMIT),
        name="proj",
    )(xf, pos3, jnp.asarray(inv), w_in, ln_z_g.reshape(-1, 1), ln_z_b.reshape(-1, 1), w_s, b_s, *later_w)

    masks = jnp.asarray(_branch_masks())
    seq_spec = pl.BlockSpec((None, None, N_RES, n_l, LANES), lambda b, hp: (hp, b, 0, 0, 0))
    state = pltpu.VMEM((N_RES, WINDOW_BLOCK, LANES), jnp.float32)
    attn = pl.pallas_call(
        _attn_kernel,
        grid=(B, N_PAIRS),
        in_specs=[seq_spec, seq_spec, seq_spec, _const_spec(masks.shape)],
        out_specs=seq_spec,
        out_shape=jax.ShapeDtypeStruct((N_PAIRS, B, N_RES, n_l, LANES), bf),
        scratch_shapes=[state] * 3,
        compiler_params=pltpu.CompilerParams(dimension_semantics=("arbitrary", "arbitrary"),
                                             vmem_limit_bytes=VMEM_LIMIT),
        name="attn",
    )(q, k, v, masks)

    Dp = p.shape[-1]
    tps3 = S // tm3
    out = pl.pallas_call(
        functools.partial(_tail_kernel, tm=tm3, ts=ts3, tiles_per_seq=tps3, alpha=alpha),
        grid=(N // tm3,),
        in_specs=[
            pl.BlockSpec((N_PAIRS, None, N_RES, tm3 // N_RES, LANES), lambda i: (0, i // tps3, 0, i % tps3, 0)),
            pl.BlockSpec((tm3, D_GMLP), lambda i: (i, 0)),
            pl.BlockSpec((tm3, D), lambda i: (i, 0)),
            pl.BlockSpec((tm3, Dp), lambda i: (i, 0)),
            _const_spec((D, D)), _const_spec((1, D)), _const_spec((1, D)),
            _const_spec((D, F)), _const_spec((D, F)), _const_spec((3, F)), _const_spec((1, F)),
            _const_spec((F, D)), _const_spec((1, D)), _const_spec((1, D)),
            _const_spec((D, D)), _const_spec((1, D)), _const_spec((Dp, D)), _const_spec((1, D)), _const_spec((1, D)),
        ],
        out_specs=pl.BlockSpec((tm3, D), lambda i: (i, 0)),
        out_shape=jax.ShapeDtypeStruct((N, D), jnp.float32),
        scratch_shapes=[pltpu.VMEM((tm3 + 8, F), jnp.float32), pltpu.VMEM((N_PAIRS, tm3, LANES), jnp.float32)],
        compiler_params=pltpu.CompilerParams(dimension_semantics=("arbitrary",), vmem_limit_bytes=VMEM_LIMIT),
        name="tail",
    )(attn, gm, xf, p.reshape(N, Dp), w_o_b, row(ln1_g), row(ln1_b),
      w_a_b, w_b_b, conv_w, row(conv_b), w_d_b, row(ln2_g), row(ln2_b),
      w_g_b, row(b_ple_gate), w_p_b, row(ln3_g), row(ln3_b))
    return out.reshape(B, S, D)


def kernel(x, p, positions, w_in, ln_z_g, ln_z_b, w_s, b_s, w_o, ln1_g, ln1_b, w_ff_a, w_ff_b, conv_w, conv_b,
           w_ff_down, ln2_g, ln2_b, w_ple_gate, b_ple_gate, w_ple_in, ln3_g, ln3_b):
    depth = w_in.shape[0]
    alpha = (2.0 * depth) ** 0.25
    for i in range(depth):
        x = _layer(x, p[i], positions, w_in[i], ln_z_g[i], ln_z_b[i], w_s[i], b_s[i], w_o[i], ln1_g[i], ln1_b[i],
                   w_ff_a[i], w_ff_b[i], conv_w[i], conv_b[i], w_ff_down[i], ln2_g[i], ln2_b[i],
                   w_ple_gate[i], b_ple_gate[i], w_ple_in[i], ln3_g[i], ln3_b[i],
                   alpha=alpha, tm1=512, ts1=256, tm3=512, ts3=256)
    return x
```

```python
import functools
import math

import numpy as np
import jax
import jax.numpy as jnp
from jax import lax
from jax.experimental import pallas as pl
from jax.experimental.pallas import tpu as pltpu

HEAD_DIM = 64
D_ATTN = 512
D_GMLP = 512
N_GROUPS = 8
CHUNK = 128
ROPE_THETA = 500000.0
ROPE_DIM = 16
LN_EPS = 1e-5
NEG_INF = -1e30
WINDOW_BLOCK = 128
LANES = 128
N_PAIRS = D_ATTN // LANES
DILATIONS = (1, 4, 16)
N_RES = DILATIONS[-1]
CAST_STEPS = 16
LOOKAHEAD = 1
VMEM_LIMIT = 56 * 1024 * 1024

_INV_SQRT2 = 0.7071067811865476


def _gelu(t):
    return 0.5 * t * (1.0 + lax.erf(t * _INV_SQRT2))


def _layernorm_rows(t, g, b):
    mu = jnp.mean(t, axis=-1, keepdims=True)
    d = t - mu
    var = jnp.mean(d * d, axis=-1, keepdims=True)
    return d * lax.rsqrt(var + LN_EPS) * g + b


def _proj_kernel(x_ref, pos_ref, inv_ref, win_ref, lnzg_ref, lnzb_ref, ws_ref, bs_ref, *rest, tm, ts, cast_scales):
    n_cast = len(cast_scales)
    cast_in, rest = rest[:n_cast], rest[n_cast:]
    q_ref, k_ref, v_ref, gm_ref = rest[:4]
    cast_out = rest[4:4 + n_cast]
    wqkv_ref, wuzT_ref, nat_s, mid_s = rest[4 + n_cast:]
    f32, bf = jnp.float32, jnp.bfloat16
    i = pl.program_id(0)

    @pl.when(i == 0)
    def _():
        wqkv_ref[...] = win_ref[:, :3 * D_ATTN].astype(bf)
        wuzT_ref[...] = win_ref[:, 3 * D_ATTN:].T.astype(bf)

    @pl.when(i < CAST_STEPS)
    def _():
        for src, dst, c in zip(cast_in, cast_out, cast_scales):
            dst[...] = (src[...] if c == 1.0 else src[...] * c).astype(bf)

    subs = [(s * ts, ts) for s in range(tm // ts)]
    xb = [x_ref[r0:r0 + n, :].astype(bf) for (r0, n) in subs]
    h_t = [lax.dot_general(wuzT_ref[...], xb[s], (((1,), (1,)), ((), ())), preferred_element_type=f32)
           for s in range(len(subs))]

    pos = pos_ref[...].astype(f32)
    ang = pos * inv_ref[...]
    cos_t = jnp.cos(ang)
    sin_t = jnp.sin(ang)
    ones48 = jnp.ones((HEAD_DIM - ROPE_DIM, tm), f32)
    zeros48 = jnp.zeros((HEAD_DIM - ROPE_DIM, tm), f32)
    c_tab = jnp.concatenate([cos_t, cos_t, ones48, cos_t, cos_t, ones48], axis=0).T
    s_tab = jnp.concatenate([-sin_t, sin_t, zeros48, -sin_t, sin_t, zeros48], axis=0).T
    lane = lax.broadcasted_iota(jnp.int32, (1, LANES), 1)
    first_half = (lane % HEAD_DIM) < (ROPE_DIM // 2)

    def rope(t, t0, n):
        up = pltpu.roll(t, LANES - ROPE_DIM // 2, 1)
        dn = pltpu.roll(t, ROPE_DIM // 2, 1)
        return t * c_tab[t0:t0 + n] + jnp.where(first_half, up, dn) * s_tab[t0:t0 + n]

    scale = math.log2(math.e) / math.sqrt(HEAD_DIM)

    def qkv_slabs(s, t0, n):
        for j in range(3 * D_ATTN // (2 * LANES)):
            hj = jnp.dot(xb[s], wqkv_ref[:, j * 2 * LANES:(j + 1) * 2 * LANES], preferred_element_type=f32)
            for half in range(2):
                idx = 2 * j + half
                t = hj[:, half * LANES:(half + 1) * LANES]
                if idx < N_PAIRS:
                    t = rope(t, t0, n) * scale
                elif idx < 2 * N_PAIRS:
                    t = rope(t, t0, n)
                nat_s[idx, t0:t0 + n, :] = t

    row = lax.broadcasted_iota(jnp.int32, (CHUNK, CHUNK), 0)
    col = lax.broadcasted_iota(jnp.int32, (CHUNK, CHUNK), 1)
    keep = col <= row
    w_sp = [jnp.where(keep, ws_ref[g], 0.0).astype(bf) for g in range(N_GROUPS)]
    n_chunks = ts // CHUNK
    for s, (t0, n) in enumerate(subs):
        qkv_slabs(s, t0, n)
        u_t = _gelu(h_t[s][:D_GMLP])
        z_t = _gelu(h_t[s][D_GMLP:])
        mu = jnp.mean(z_t, axis=0, keepdims=True)
        d = z_t - mu
        var = jnp.mean(d * d, axis=0, keepdims=True)
        zn_t = (d * lax.rsqrt(var + LN_EPS) * lnzg_ref[...] + lnzb_ref[...]).astype(bf)
        mixed = []
        for g in range(N_GROUPS):
            r0 = g * HEAD_DIM
            lhs = jnp.concatenate([zn_t[r0:r0 + HEAD_DIM, c * CHUNK:(c + 1) * CHUNK] for c in range(n_chunks)],
                                  axis=0)
            res = lax.dot_general(lhs, w_sp[g], (((1,), (1,)), ((), ())),
                                  preferred_element_type=f32) + bs_ref[g:g + 1, :]
            mixed.append(jnp.concatenate([res[c * HEAD_DIM:(c + 1) * HEAD_DIM] for c in range(n_chunks)],
                                         axis=1))
        mixed_t = jnp.concatenate(mixed, axis=0)
        gm_ref[t0:t0 + n, :] = (u_t * mixed_t).T.astype(bf)

    quarter = tm // 4
    n_l = tm // N_RES
    for idx in range(3 * N_PAIRS):
        for r in range(4):
            mid_s[idx, r * quarter:(r + 1) * quarter, :] = nat_s[idx, pl.ds(r, quarter, stride=4), :]
    for which, o_ref in enumerate((q_ref, k_ref, v_ref)):
        for hp in range(N_PAIRS):
            idx = which * N_PAIRS + hp
            for r_lo in range(4):
                for r_hi in range(4):
                    o_ref[hp, 4 * r_hi + r_lo] = mid_s[idx, pl.ds(r_lo * quarter + r_hi, n_l, stride=4), :]


def _branch_masks():
    w = WINDOW_BLOCK
    out = np.zeros((2 * len(DILATIONS), w, 2 * w), np.float32)
    for c, d in enumerate(DILATIONS):
        rows = w * d // N_RES
        aq, i = np.divmod(np.arange(w), rows)
        ak, j = np.divmod(np.arange(2 * w), 2 * rows)
        base = (N_RES // d) * (i[:, None] - j[None, :]) + (aq[:, None] - ak[None, :])
        dist = base + w
        out[2 * c] = (dist >= 0) & (dist <= w)
        dist0 = base
        out[2 * c + 1] = (dist0 >= 0) & (dist0 <= w)
    return out


def _attn_kernel(q_s, k_s, v_s, mask_ref, o_ref, acc_s, m_s, l_s):
    n_l = q_s.shape[1]
    n_spans = n_l // WINDOW_BLOCK

    lane = lax.broadcasted_iota(jnp.int32, (1, LANES), 1)
    head0 = lane < HEAD_DIM

    def gather(ref, chunks):
        return jnp.concatenate([ref[r, pl.ds(st, n), :] for (r, st, n) in chunks], axis=0)

    def run_chains(chains):
        w = WINDOW_BLOCK

        def scores(c):
            q_chunks, kv_chunks = chains[c][:2]
            qv = gather(q_s, q_chunks)
            kb = gather(k_s, kv_chunks).astype(jnp.bfloat16)
            q2 = jnp.concatenate([jnp.where(head0, qv, 0.0), jnp.where(head0, 0.0, qv)],
                                 axis=0).astype(jnp.bfloat16)
            return lax.dot_general(q2, kb, (((1,), (1,)), ((), ())),
                                   preferred_element_type=jnp.float32)

        def finish(c, s2):
            q_chunks, kv_chunks, st_chunks, mask_idx, first, last = chains[c]
            valid = mask_ref[mask_idx] > 0.5
            m_old = None if first else gather(m_s, st_chunks)
            ps, ms = [], []
            for hh in range(2):
                s = jnp.where(valid, s2[hh * w:(hh + 1) * w], NEG_INF)
                t = jnp.maximum(s[:, :LANES], s[:, LANES:])
                if not first:
                    own = head0 if hh == 0 else jnp.logical_not(head0)
                    t = jnp.maximum(t, jnp.where(own, m_old, NEG_INF))
                m_new = jnp.max(t, axis=1, keepdims=True)
                ps.append(jnp.exp2(s - m_new))
                ms.append(m_new)
            p2 = jnp.concatenate(ps, axis=0).astype(jnp.bfloat16)
            m_b = jnp.where(head0, ms[0], ms[1])
            vb = gather(v_s, kv_chunks).astype(jnp.bfloat16)
            vb1 = jnp.concatenate([vb, jnp.ones_like(vb)], axis=1)
            o2 = jnp.dot(p2, vb1, preferred_element_type=jnp.float32)
            o_pair = jnp.where(head0, o2[:w, :LANES], o2[w:, :LANES])
            l_b = jnp.where(head0, o2[:w, LANES:], o2[w:, LANES:])
            if first:
                acc, l_tot = o_pair, l_b
            else:
                alpha = jnp.exp2(m_old - m_b)
                acc = gather(acc_s, st_chunks) * alpha + o_pair
                l_tot = gather(l_s, st_chunks) * alpha + l_b
            if last:
                (r, st, n), = q_chunks
                o_ref[r, pl.ds(st, n), :] = (acc / l_tot).astype(o_ref.dtype)
            else:
                off = 0
                for (r, st, n) in st_chunks:
                    acc_s[r, pl.ds(st, n), :] = acc[off:off + n]
                    m_s[r, pl.ds(st, n), :] = m_b[off:off + n]
                    l_s[r, pl.ds(st, n), :] = l_tot[off:off + n]
                    off += n

        pending = {c: scores(c) for c in range(min(LOOKAHEAD, len(chains)))}
        for c in range(len(chains)):
            s2 = pending.pop(c) if c in pending else scores(c)
            if LOOKAHEAD and c + LOOKAHEAD < len(chains):
                pending[c + LOOKAHEAD] = scores(c + LOOKAHEAD)
            finish(c, s2)

    chains = []
    for span in range(n_spans):
        for c, d in enumerate(DILATIONS):
            rows, per_span = WINDOW_BLOCK * d // N_RES, N_RES // d
            for r_d in range(d):
                pieces = [r_d + d * a for a in range(N_RES // d)]
                for j in range(per_span):
                    n = span * per_span + j
                    q0, k0, s0 = n * rows, max(n - 1, 0) * rows, j * rows
                    chains.append(([(r, q0, rows) for r in pieces], [(r, k0, 2 * rows) for r in pieces],
                                   [(r, s0, rows) for r in pieces], 2 * c + (1 if n == 0 else 0),
                                   c == 0, c == len(DILATIONS) - 1))
    run_chains(chains)


def _tail_kernel(attn_ref, gm_ref, x_ref, p_ref, wo_ref, ln1g_ref, ln1b_ref, wa_ref, wb_ref, cw_ref, cb_ref,
                 wd_ref, ln2g_ref, ln2b_ref, wg_ref, bg_ref, wp_ref, ln3g_ref, ln3b_ref, o_ref, a_s, un_s,
                 *, tm, ts, tiles_per_seq, alpha):
    i = pl.program_id(0)

    @pl.when(i % tiles_per_seq == 0)
    def _():
        a_s[0:8, :] = jnp.zeros((8, a_s.shape[1]), jnp.float32)

    subs = [(s * ts, ts) for s in range(tm // ts)]
    f32, bf = jnp.float32, jnp.bfloat16
    mm = lambda a, w_ref: jnp.dot(a, w_ref[...], preferred_element_type=f32)

    pe = [mm(p_ref[r0:r0 + n, :].astype(bf), wp_ref) for (r0, n) in subs]

    n_l = tm // N_RES
    for hp in range(N_PAIRS):
        for r in range(N_RES):
            un_s[hp, pl.ds(r, n_l, stride=N_RES), :] = attn_ref[hp, r].astype(f32)

    mix = []
    for (r0, n) in subs:
        mixin = jnp.concatenate([un_s[hp, r0:r0 + n, :].astype(bf) for hp in range(N_PAIRS)]
                                + [gm_ref[r0:r0 + n, :]], axis=1)
        mix.append(mm(mixin, wo_ref))
    g1, b1 = ln1g_ref[...] * alpha, ln1b_ref[...] * alpha
    g2, b2 = ln2g_ref[...] * alpha, ln2b_ref[...] * alpha
    cw = cw_ref[...] * _INV_SQRT2
    cb = cb_ref[...] * _INV_SQRT2
    r1 = [_layernorm_rows(alpha * x_ref[r0:r0 + n, :] + mix[s], g1, b1) for s, (r0, n) in enumerate(subs)]
    r1b = [t.astype(bf) for t in r1]
    hb = []
    for s, (r0, n) in enumerate(subs):
        a_s[8 + r0:8 + r0 + n, :] = mm(r1b[s], wa_ref)
        hb.append(mm(r1b[s], wb_ref))
    ff = []
    for s, (r0, n) in enumerate(subs):
        conv = cb + cw[0:1, :] * a_s[6 + r0:6 + r0 + n, :]
        conv = conv + cw[1:2, :] * a_s[7 + r0:7 + r0 + n, :]
        conv = conv + cw[2:3, :] * a_s[8 + r0:8 + r0 + n, :]
        ff.append(mm((conv * (1.0 + lax.erf(conv)) * hb[s]).astype(bf), wd_ref))
    a_s[0:8, :] = a_s[tm:tm + 8, :]
    r2 = [_layernorm_rows(r1[s] + ff[s], g2, b2) for s in range(len(subs))]
    gate = [jax.nn.sigmoid(mm(r2[s].astype(bf), wg_ref) + bg_ref[...]) for s in range(len(subs))]
    for s, (r0, n) in enumerate(subs):
        o_ref[r0:r0 + n, :] = _layernorm_rows(r2[s] + gate[s] * pe[s], ln3g_ref[...], ln3b_ref[...])


def _const_spec(shape):
    nd = len(shape)
    return pl.BlockSpec(shape, lambda *_: (0,) * nd, pipeline_mode=pl.Buffered(1))


def _layer(x, p, positions, w_in, ln_z_g, ln_z_b, w_s, b_s, w_o, ln1_g, ln1_b, w_ff_a, w_ff_b, conv_w, conv_b,
           w_ff_down, ln2_g, ln2_b, w_ple_gate, b_ple_gate, w_ple_in, ln3_g, ln3_b, *, alpha, tm1, ts1, tm3, ts3):
    B, S, D = x.shape
    N = B * S
    F = w_ff_a.shape[1]
    bf = jnp.bfloat16
    xf = x.reshape(N, D)

    inv = np.float32(ROPE_THETA ** (-np.arange(0, ROPE_DIM, 2, dtype=np.float64) / ROPE_DIM)).reshape(8, 1)
    pos3 = positions.reshape(N // tm1, 1, tm1)
    tps1 = S // tm1
    n_l = S // N_RES
    row = lambda v: v.reshape(1, -1)
    later_w = [w_o, w_ff_a, w_ff_b, w_ff_down, w_ple_gate, w_ple_in]
    later_scale = (1.0, 1.0 / alpha, 1.0 / alpha, 0.5 / _INV_SQRT2, 1.0 / alpha, 1.0)
    slab = lambda w: pl.BlockSpec((w.shape[0] // CAST_STEPS, w.shape[1]),
                                  lambda i: (jnp.minimum(i, CAST_STEPS - 1), 0))
    q, k, v, gm, w_o_b, w_a_b, w_b_b, w_d_b, w_g_b, w_p_b = pl.pallas_call(
        functools.partial(_proj_kernel, tm=tm1, ts=ts1, cast_scales=later_scale),
        grid=(N // tm1,),
        in_specs=[
            pl.BlockSpec((tm1, D), lambda i: (i, 0)),
            pl.BlockSpec((None, 1, tm1), lambda i: (i, 0, 0)),
            _const_spec((8, 1)),
            _const_spec(w_in.shape),
            _const_spec((D_GMLP, 1)),
            _const_spec((D_GMLP, 1)),
            _const_spec((N_GROUPS, CHUNK, CHUNK)),
            _const_spec((N_GROUPS, CHUNK)),
        ] + [slab(w) for w in later_w],
        out_specs=[pl.BlockSpec((N_PAIRS, None, N_RES, tm1 // N_RES, LANES),
                                lambda i: (0, i // tps1, 0, i % tps1, 0))] * 3
        + [pl.BlockSpec((tm1, D_GMLP), lambda i: (i, 0))] + [slab(w) for w in later_w],
        out_shape=[jax.ShapeDtypeStruct((N_PAIRS, B, N_RES, n_l, LANES), jnp.float32)] * 3
        + [jax.ShapeDtypeStruct((N, D_GMLP), bf)] + [jax.ShapeDtypeStruct(w.shape, bf) for w in later_w],
        scratch_shapes=[pltpu.VMEM((D, 3 * D_ATTN), bf), pltpu.VMEM((2 * D_GMLP, D), bf)]
        + [pltpu.VMEM((3 * N_PAIRS, tm1, LANES), jnp.float32)] * 2,
        compiler_params=pltpu.CompilerParams(dimension_semantics=("arbitrary",), vmem_limit_bytes=VMEM_LIMIT),
        name="proj",
    )(xf, pos3, jnp.asarray(inv), w_in, ln_z_g.reshape(-1, 1), ln_z_b.reshape(-1, 1), w_s, b_s, *later_w)

    masks = jnp.asarray(_branch_masks())
    seq_spec = pl.BlockSpec((None, None, N_RES, n_l, LANES), lambda b, hp: (hp, b, 0, 0, 0))
    state = pltpu.VMEM((N_RES, WINDOW_BLOCK, LANES), jnp.float32)
    attn = pl.pallas_call(
        _attn_kernel,
        grid=(B, N_PAIRS),
        in_specs=[seq_spec, seq_spec, seq_spec, _const_spec(masks.shape)],
        out_specs=seq_spec,
        out_shape=jax.ShapeDtypeStruct((N_PAIRS, B, N_RES, n_l, LANES), bf),
        scratch_shapes=[state] * 3,
        compiler_params=pltpu.CompilerParams(dimension_semantics=("arbitrary", "arbitrary"),
                                             vmem_limit_bytes=VMEM_LIMIT),
        name="attn",
    )(q, k, v, masks)

    Dp = p.shape[-1]
    tps3 = S // tm3
    out = pl.pallas_call(
        functools.partial(_tail_kernel, tm=tm3, ts=ts3, tiles_per_seq=tps3, alpha=alpha),
        grid=(N // tm3,),
        in_specs=[
            pl.BlockSpec((N_PAIRS, None, N_RES, tm3 // N_RES, LANES), lambda i: (0, i // tps3, 0, i % tps3, 0)),
            pl.BlockSpec((tm3, D_GMLP), lambda i: (i, 0)),
            pl.BlockSpec((tm3, D), lambda i: (i, 0)),
            pl.BlockSpec((tm3, Dp), lambda i: (i, 0)),
            _const_spec((D, D)), _const_spec((1, D)), _const_spec((1, D)),
            _const_spec((D, F)), _const_spec((D, F)), _const_spec((3, F)), _const_spec((1, F)),
            _const_spec((F, D)), _const_spec((1, D)), _const_spec((1, D)),
            _const_spec((D, D)), _const_spec((1, D)), _const_spec((Dp, D)), _const_spec((1, D)), _const_spec((1, D)),
        ],
        out_specs=pl.BlockSpec((tm3, D), lambda i: (i, 0)),
        out_shape=jax.ShapeDtypeStruct((N, D), jnp.float32),
        scratch_shapes=[pltpu.VMEM((tm3 + 8, F), jnp.float32), pltpu.VMEM((N_PAIRS, tm3, LANES), jnp.float32)],
        compiler_params=pltpu.CompilerParams(dimension_semantics=("arbitrary",), vmem_limit_bytes=VMEM_LIMIT),
        name="tail",
    )(attn, gm, xf, p.reshape(N, Dp), w_o_b, row(ln1_g), row(ln1_b),
      w_a_b, w_b_b, conv_w, row(conv_b), w_d_b, row(ln2_g), row(ln2_b),
      w_g_b, row(b_ple_gate), w_p_b, row(ln3_g), row(ln3_b))
    return out.reshape(B, S, D)


def kernel(x, p, positions, w_in, ln_z_g, ln_z_b, w_s, b_s, w_o, ln1_g, ln1_b, w_ff_a, w_ff_b, conv_w, conv_b,
           w_ff_down, ln2_g, ln2_b, w_ple_gate, b_ple_gate, w_ple_in, ln3_g, ln3_b):
    depth = w_in.shape[0]
    alpha = (2.0 * depth) ** 0.25
    for i in range(depth):
        x = _layer(x, p[i], positions, w_in[i], ln_z_g[i], ln_z_b[i], w_s[i], b_s[i], w_o[i], ln1_g[i], ln1_b[i],
                   w_ff_a[i], w_ff_b[i], conv_w[i], conv_b[i], w_ff_down[i], ln2_g[i], ln2_b[i],
                   w_ple_gate[i], b_ple_gate[i], w_ple_in[i], ln3_g[i], ln3_b[i],
                   alpha=alpha, tm1=512, ts1=256, tm3=512, ts3=256)
    return x
```

```python
import functools
import math

import numpy as np
import jax
import jax.numpy as jnp
from jax import lax
from jax.experimental import pallas as pl
from jax.experimental.pallas import tpu as pltpu

HEAD_DIM = 64
D_ATTN = 512
D_GMLP = 512
N_GROUPS = 8
CHUNK = 128
ROPE_THETA = 500000.0
ROPE_DIM = 16
LN_EPS = 1e-5
NEG_INF = -1e30
WINDOW_BLOCK = 128
LANES = 128
N_PAIRS = D_ATTN // LANES
DILATIONS = (1, 4, 16)
N_RES = DILATIONS[-1]
CAST_STEPS = 16
LOOKAHEAD = 1
VMEM_LIMIT = 56 * 1024 * 1024

_INV_SQRT2 = 0.7071067811865476


def _gelu(t):
    return 0.5 * t * (1.0 + lax.erf(t * _INV_SQRT2))


def _layernorm_rows(t, g, b):
    mu = jnp.mean(t, axis=-1, keepdims=True)
    d = t - mu
    var = jnp.mean(d * d, axis=-1, keepdims=True)
    return d * lax.rsqrt(var + LN_EPS) * g + b


def _proj_kernel(x_ref, pos_ref, inv_ref, win_ref, lnzg_ref, lnzb_ref, ws_ref, bs_ref, *rest, tm, ts, cast_scales):
    n_cast = len(cast_scales)
    cast_in, rest = rest[:n_cast], rest[n_cast:]
    q_ref, k_ref, v_ref, gm_ref = rest[:4]
    cast_out = rest[4:4 + n_cast]
    wqkv_ref, wuzT_ref, nat_s, mid_s = rest[4 + n_cast:]
    f32, bf = jnp.float32, jnp.bfloat16
    i = pl.program_id(0)

    @pl.when(i == 0)
    def _():
        wqkv_ref[...] = win_ref[:, :3 * D_ATTN].astype(bf)
        wuzT_ref[...] = win_ref[:, 3 * D_ATTN:].T.astype(bf)

    @pl.when(i < CAST_STEPS)
    def _():
        for src, dst, c in zip(cast_in, cast_out, cast_scales):
            dst[...] = (src[...] if c == 1.0 else src[...] * c).astype(bf)

    subs = [(s * ts, ts) for s in range(tm // ts)]
    xb = [x_ref[r0:r0 + n, :].astype(bf) for (r0, n) in subs]
    h_t = [lax.dot_general(wuzT_ref[...], xb[s], (((1,), (1,)), ((), ())), preferred_element_type=f32)
           for s in range(len(subs))]

    pos = pos_ref[...].astype(f32)
    ang = pos * inv_ref[...]
    cos_t = jnp.cos(ang)
    sin_t = jnp.sin(ang)
    ones48 = jnp.ones((HEAD_DIM - ROPE_DIM, tm), f32)
    zeros48 = jnp.zeros((HEAD_DIM - ROPE_DIM, tm), f32)
    c_tab = jnp.concatenate([cos_t, cos_t, ones48, cos_t, cos_t, ones48], axis=0).T
    s_tab = jnp.concatenate([-sin_t, sin_t, zeros48, -sin_t, sin_t, zeros48], axis=0).T
    lane = lax.broadcasted_iota(jnp.int32, (1, LANES), 1)
    first_half = (lane % HEAD_DIM) < (ROPE_DIM // 2)

    def rope(t, t0, n):
        up = pltpu.roll(t, LANES - ROPE_DIM // 2, 1)
        dn = pltpu.roll(t, ROPE_DIM // 2, 1)
        return t * c_tab[t0:t0 + n] + jnp.where(first_half, up, dn) * s_tab[t0:t0 + n]

    scale = math.log2(math.e) / math.sqrt(HEAD_DIM)

    def qkv_slabs(s, t0, n):
        for j in range(3 * D_ATTN // (2 * LANES)):
            hj = jnp.dot(xb[s], wqkv_ref[:, j * 2 * LANES:(j + 1) * 2 * LANES], preferred_element_type=f32)
            for half in range(2):
                idx = 2 * j + half
                t = hj[:, half * LANES:(half + 1) * LANES]
                if idx < N_PAIRS:
                    t = rope(t, t0, n) * scale
                elif idx < 2 * N_PAIRS:
                    t = rope(t, t0, n)
                nat_s[idx, t0:t0 + n, :] = t

    row = lax.broadcasted_iota(jnp.int32, (CHUNK, CHUNK), 0)
    col = lax.broadcasted_iota(jnp.int32, (CHUNK, CHUNK), 1)
    keep = col <= row
    w_sp = [jnp.where(keep, ws_ref[g], 0.0).astype(bf) for g in range(N_GROUPS)]
    n_chunks = ts // CHUNK
    for s, (t0, n) in enumerate(subs):
        qkv_slabs(s, t0, n)
        u_t = _gelu(h_t[s][:D_GMLP])
        z_t = _gelu(h_t[s][D_GMLP:])
        mu = jnp.mean(z_t, axis=0, keepdims=True)
        d = z_t - mu
        var = jnp.mean(d * d, axis=0, keepdims=True)
        zn_t = (d * lax.rsqrt(var + LN_EPS) * lnzg_ref[...] + lnzb_ref[...]).astype(bf)
        mixed = []
        for g in range(N_GROUPS):
            r0 = g * HEAD_DIM
            lhs = jnp.concatenate([zn_t[r0:r0 + HEAD_DIM, c * CHUNK:(c + 1) * CHUNK] for c in range(n_chunks)],
                                  axis=0)
            res = lax.dot_general(lhs, w_sp[g], (((1,), (1,)), ((), ())),
                                  preferred_element_type=f32) + bs_ref[g:g + 1, :]
            mixed.append(jnp.concatenate([res[c * HEAD_DIM:(c + 1) * HEAD_DIM] for c in range(n_chunks)],
                                         axis=1))
        mixed_t = jnp.concatenate(mixed, axis=0)
        gm_ref[t0:t0 + n, :] = (u_t * mixed_t).T.astype(bf)

    quarter = tm // 4
    n_l = tm // N_RES
    for idx in range(3 * N_PAIRS):
        for r in range(4):
            mid_s[idx, r * quarter:(r + 1) * quarter, :] = nat_s[idx, pl.ds(r, quarter, stride=4), :]
    for which, o_ref in enumerate((q_ref, k_ref, v_ref)):
        for hp in range(N_PAIRS):
            idx = which * N_PAIRS + hp
            for r_lo in range(4):
                for r_hi in range(4):
                    o_ref[hp, 4 * r_hi + r_lo] = mid_s[idx, pl.ds(r_lo * quarter + r_hi, n_l, stride=4), :]


def _branch_masks():
    w = WINDOW_BLOCK
    out = np.zeros((2 * len(DILATIONS), w, 2 * w), np.float32)
    for c, d in enumerate(DILATIONS):
        rows = w * d // N_RES
        aq, i = np.divmod(np.arange(w), rows)
        ak, j = np.divmod(np.arange(2 * w), 2 * rows)
        base = (N_RES // d) * (i[:, None] - j[None, :]) + (aq[:, None] - ak[None, :])
        dist = base + w
        out[2 * c] = (dist >= 0) & (dist <= w)
        dist0 = base
        out[2 * c + 1] = (dist0 >= 0) & (dist0 <= w)
    return out


def _attn_kernel(q_s, k_s, v_s, mask_ref, o_ref, acc_s, m_s, l_s):
    n_l = q_s.shape[1]
    n_spans = n_l // WINDOW_BLOCK

    lane = lax.broadcasted_iota(jnp.int32, (1, LANES), 1)
    head0 = lane < HEAD_DIM

    def gather(ref, chunks):
        return jnp.concatenate([ref[r, pl.ds(st, n), :] for (r, st, n) in chunks], axis=0)

    def run_chains(chains):
        w = WINDOW_BLOCK

        def scores(c):
            q_chunks, kv_chunks = chains[c][:2]
            qv = gather(q_s, q_chunks)
            kb = gather(k_s, kv_chunks).astype(jnp.bfloat16)
            q2 = jnp.concatenate([jnp.where(head0, qv, 0.0), jnp.where(head0, 0.0, qv)],
                                 axis=0).astype(jnp.bfloat16)
            return lax.dot_general(q2, kb, (((1,), (1,)), ((), ())),
                                   preferred_element_type=jnp.float32)

        def finish(c, s2):
            q_chunks, kv_chunks, st_chunks, mask_idx, first, last = chains[c]
            valid = mask_ref[mask_idx] > 0.5
            m_old = None if first else gather(m_s, st_chunks)
            ps, ms = [], []
            for hh in range(2):
                s = jnp.where(valid, s2[hh * w:(hh + 1) * w], NEG_INF)
                t = jnp.maximum(s[:, :LANES], s[:, LANES:])
                if not first:
                    own = head0 if hh == 0 else jnp.logical_not(head0)
                    t = jnp.maximum(t, jnp.where(own, m_old, NEG_INF))
                m_new = jnp.max(t, axis=1, keepdims=True)
                ps.append(jnp.exp2(s - m_new))
                ms.append(m_new)
            p2 = jnp.concatenate(ps, axis=0).astype(jnp.bfloat16)
            m_b = jnp.where(head0, ms[0], ms[1])
            vb = gather(v_s, kv_chunks).astype(jnp.bfloat16)
            vb1 = jnp.concatenate([vb, jnp.ones_like(vb)], axis=1)
            o2 = jnp.dot(p2, vb1, preferred_element_type=jnp.float32)
            o_pair = jnp.where(head0, o2[:w, :LANES], o2[w:, :LANES])
            l_b = jnp.where(head0, o2[:w, LANES:], o2[w:, LANES:])
            if first:
                acc, l_tot = o_pair, l_b
            else:
                alpha = jnp.exp2(m_old - m_b)
                acc = gather(acc_s, st_chunks) * alpha + o_pair
                l_tot = gather(l_s, st_chunks) * alpha + l_b
            if last:
                (r, st, n), = q_chunks
                o_ref[r, pl.ds(st, n), :] = (acc / l_tot).astype(o_ref.dtype)
            else:
                off = 0
                for (r, st, n) in st_chunks:
                    acc_s[r, pl.ds(st, n), :] = acc[off:off + n]
                    m_s[r, pl.ds(st, n), :] = m_b[off:off + n]
                    l_s[r, pl.ds(st, n), :] = l_tot[off:off + n]
                    off += n

        pending = {c: scores(c) for c in range(min(LOOKAHEAD, len(chains)))}
        for c in range(len(chains)):
            s2 = pending.pop(c) if c in pending else scores(c)
            if LOOKAHEAD and c + LOOKAHEAD < len(chains):
                pending[c + LOOKAHEAD] = scores(c + LOOKAHEAD)
            finish(c, s2)

    chains = []
    for span in range(n_spans):
        for c, d in enumerate(DILATIONS):
            rows, per_span = WINDOW_BLOCK * d // N_RES, N_RES // d
            for r_d in range(d):
                pieces = [r_d + d * a for a in range(N_RES // d)]
                for j in range(per_span):
                    n = span * per_span + j
                    q0, k0, s0 = n * rows, max(n - 1, 0) * rows, j * rows
                    chains.append(([(r, q0, rows) for r in pieces], [(r, k0, 2 * rows) for r in pieces],
                                   [(r, s0, rows) for r in pieces], 2 * c + (1 if n == 0 else 0),
                                   c == 0, c == len(DILATIONS) - 1))
    run_chains(chains)


def _tail_kernel(attn_ref, gm_ref, x_ref, p_ref, wo_ref, ln1g_ref, ln1b_ref, wa_ref, wb_ref, cw_ref, cb_ref,
                 wd_ref, ln2g_ref, ln2b_ref, wg_ref, bg_ref, wp_ref, ln3g_ref, ln3b_ref, o_ref, a_s, un_s, um_s,
                 *, tm, ts, tiles_per_seq, alpha):
    i = pl.program_id(0)

    @pl.when(i % tiles_per_seq == 0)
    def _():
        a_s[0:8, :] = jnp.zeros((8, a_s.shape[1]), jnp.float32)

    subs = [(s * ts, ts) for s in range(tm // ts)]
    f32, bf = jnp.float32, jnp.bfloat16
    mm = lambda a, w_ref: jnp.dot(a, w_ref[...], preferred_element_type=f32)

    pe = [mm(p_ref[r0:r0 + n, :].astype(bf), wp_ref) for (r0, n) in subs]

    n_l, quarter = tm // N_RES, tm // 4
    for hp in range(N_PAIRS):
        for r_lo in range(4):
            for r_hi in range(4):
                um_s[hp, pl.ds(r_lo * quarter + r_hi, n_l, stride=4), :] = attn_ref[hp, 4 * r_hi + r_lo].astype(f32)
            un_s[hp, pl.ds(r_lo, quarter, stride=4), :] = um_s[hp, r_lo * quarter:(r_lo + 1) * quarter, :]

    mix = []
    for (r0, n) in subs:
        mixin = jnp.concatenate([un_s[hp, r0:r0 + n, :].astype(bf) for hp in range(N_PAIRS)]
                                + [gm_ref[r0:r0 + n, :]], axis=1)
        mix.append(mm(mixin, wo_ref))
    g1, b1 = ln1g_ref[...] * alpha, ln1b_ref[...] * alpha
    g2, b2 = ln2g_ref[...] * alpha, ln2b_ref[...] * alpha
    cw = cw_ref[...] * _INV_SQRT2
    cb = cb_ref[...] * _INV_SQRT2
    r1 = [_layernorm_rows(alpha * x_ref[r0:r0 + n, :] + mix[s], g1, b1) for s, (r0, n) in enumerate(subs)]
    r1b = [t.astype(bf) for t in r1]
    hb = []
    for s, (r0, n) in enumerate(subs):
        a_s[8 + r0:8 + r0 + n, :] = mm(r1b[s], wa_ref)
        hb.append(mm(r1b[s], wb_ref))
    ff = []
    for s, (r0, n) in enumerate(subs):
        conv = cb + cw[0:1, :] * a_s[6 + r0:6 + r0 + n, :]
        conv = conv + cw[1:2, :] * a_s[7 + r0:7 + r0 + n, :]
        conv = conv + cw[2:3, :] * a_s[8 + r0:8 + r0 + n, :]
        ff.append(mm((conv * (1.0 + lax.erf(conv)) * hb[s]).astype(bf), wd_ref))
    a_s[0:8, :] = a_s[tm:tm + 8, :]
    r2 = [_layernorm_rows(r1[s] + ff[s], g2, b2) for s in range(len(subs))]
    gate = [jax.nn.sigmoid(mm(r2[s].astype(bf), wg_ref) + bg_ref[...]) for s in range(len(subs))]
    for s, (r0, n) in enumerate(subs):
        o_ref[r0:r0 + n, :] = _layernorm_rows(r2[s] + gate[s] * pe[s], ln3g_ref[...], ln3b_ref[...])


def _const_spec(shape):
    nd = len(shape)
    return pl.BlockSpec(shape, lambda *_: (0,) * nd, pipeline_mode=pl.Buffered(1))


def _layer(x, p, positions, w_in, ln_z_g, ln_z_b, w_s, b_s, w_o, ln1_g, ln1_b, w_ff_a, w_ff_b, conv_w, conv_b,
           w_ff_down, ln2_g, ln2_b, w_ple_gate, b_ple_gate, w_ple_in, ln3_g, ln3_b, *, alpha, tm1, ts1, tm3, ts3):
    B, S, D = x.shape
    N = B * S
    F = w_ff_a.shape[1]
    bf = jnp.bfloat16
    xf = x.reshape(N, D)

    inv = np.float32(ROPE_THETA ** (-np.arange(0, ROPE_DIM, 2, dtype=np.float64) / ROPE_DIM)).reshape(8, 1)
    pos3 = positions.reshape(N // tm1, 1, tm1)
    tps1 = S // tm1
    n_l = S // N_RES
    row = lambda v: v.reshape(1, -1)
    later_w = [w_o, w_ff_a, w_ff_b, w_ff_down, w_ple_gate, w_ple_in]
    later_scale = (1.0, 1.0 / alpha, 1.0 / alpha, 0.5 / _INV_SQRT2, 1.0 / alpha, 1.0)
    slab = lambda w: pl.BlockSpec((w.shape[0] // CAST_STEPS, w.shape[1]),
                                  lambda i: (jnp.minimum(i, CAST_STEPS - 1), 0))
    q, k, v, gm, w_o_b, w_a_b, w_b_b, w_d_b, w_g_b, w_p_b = pl.pallas_call(
        functools.partial(_proj_kernel, tm=tm1, ts=ts1, cast_scales=later_scale),
        grid=(N // tm1,),
        in_specs=[
            pl.BlockSpec((tm1, D), lambda i: (i, 0)),
            pl.BlockSpec((None, 1, tm1), lambda i: (i, 0, 0)),
            _const_spec((8, 1)),
            _const_spec(w_in.shape),
            _const_spec((D_GMLP, 1)),
            _const_spec((D_GMLP, 1)),
            _const_spec((N_GROUPS, CHUNK, CHUNK)),
            _const_spec((N_GROUPS, CHUNK)),
        ] + [slab(w) for w in later_w],
        out_specs=[pl.BlockSpec((N_PAIRS, None, N_RES, tm1 // N_RES, LANES),
                                lambda i: (0, i // tps1, 0, i % tps1, 0))] * 3
        + [pl.BlockSpec((tm1, D_GMLP), lambda i: (i, 0))] + [slab(w) for w in later_w],
        out_shape=[jax.ShapeDtypeStruct((N_PAIRS, B, N_RES, n_l, LANES), jnp.float32)] * 3
        + [jax.ShapeDtypeStruct((N, D_GMLP), bf)] + [jax.ShapeDtypeStruct(w.shape, bf) for w in later_w],
        scratch_shapes=[pltpu.VMEM((D, 3 * D_ATTN), bf), pltpu.VMEM((2 * D_GMLP, D), bf)]
        + [pltpu.VMEM((3 * N_PAIRS, tm1, LANES), jnp.float32)] * 2,
        compiler_params=pltpu.CompilerParams(dimension_semantics=("arbitrary",), vmem_limit_bytes=VMEM_LIMIT),
        name="proj",
    )(xf, pos3, jnp.asarray(inv), w_in, ln_z_g.reshape(-1, 1), ln_z_b.reshape(-1, 1), w_s, b_s, *later_w)

    masks = jnp.asarray(_branch_masks())
    seq_spec = pl.BlockSpec((None, None, N_RES, n_l, LANES), lambda b, hp: (hp, b, 0, 0, 0))
    state = pltpu.VMEM((N_RES, WINDOW_BLOCK, LANES), jnp.float32)
    attn = pl.pallas_call(
        _attn_kernel,
        grid=(B, N_PAIRS),
        in_specs=[seq_spec, seq_spec, seq_spec, _const_spec(masks.shape)],
        out_specs=seq_spec,
        out_shape=jax.ShapeDtypeStruct((N_PAIRS, B, N_RES, n_l, LANES), bf),
        scratch_shapes=[state] * 3,
        compiler_params=pltpu.CompilerParams(dimension_semantics=("arbitrary", "arbitrary"),
                                             vmem_limit_bytes=VMEM_LIMIT),
        name="attn",
    )(q, k, v, masks)

    Dp = p.shape[-1]
    tps3 = S // tm3
    out = pl.pallas_call(
        functools.partial(_tail_kernel, tm=tm3, ts=ts3, tiles_per_seq=tps3, alpha=alpha),
        grid=(N // tm3,),
        in_specs=[
            pl.BlockSpec((N_PAIRS, None, N_RES, tm3 // N_RES, LANES), lambda i: (0, i // tps3, 0, i % tps3, 0)),
            pl.BlockSpec((tm3, D_GMLP), lambda i: (i, 0)),
            pl.BlockSpec((tm3, D), lambda i: (i, 0)),
            pl.BlockSpec((tm3, Dp), lambda i: (i, 0)),
            _const_spec((D, D)), _const_spec((1, D)), _const_spec((1, D)),
            _const_spec((D, F)), _const_spec((D, F)), _const_spec((3, F)), _const_spec((1, F)),
            _const_spec((F, D)), _const_spec((1, D)), _const_spec((1, D)),
            _const_spec((D, D)), _const_spec((1, D)), _const_spec((Dp, D)), _const_spec((1, D)), _const_spec((1, D)),
        ],
        out_specs=pl.BlockSpec((tm3, D), lambda i: (i, 0)),
        out_shape=jax.ShapeDtypeStruct((N, D), jnp.float32),
        scratch_shapes=[pltpu.VMEM((tm3 + 8, F), jnp.float32)] + [pltpu.VMEM((N_PAIRS, tm3, LANES), jnp.float32)] * 2,
        compiler_params=pltpu.CompilerParams(dimension_semantics=("arbitrary",), vmem_limit_bytes=VMEM_LIMIT),
        name="tail",
    )(attn, gm, xf, p.reshape(N, Dp), w_o_b, row(ln1_g), row(ln1_b),
      w_a_b, w_b_b, conv_w, row(conv_b), w_d_b, row(ln2_g), row(ln2_b),
      w_g_b, row(b_ple_gate), w_p_b, row(ln3_g), row(ln3_b))
    return out.reshape(B, S, D)


def kernel(x, p, positions, w_in, ln_z_g, ln_z_b, w_s, b_s, w_o, ln1_g, ln1_b, w_ff_a, w_ff_b, conv_w, conv_b,
           w_ff_down, ln2_g, ln2_b, w_ple_gate, b_ple_gate, w_ple_in, ln3_g, ln3_b):
    depth = w_in.shape[0]
    alpha = (2.0 * depth) ** 0.25
    for i in range(depth):
        x = _layer(x, p[i], positions, w_in[i], ln_z_g[i], ln_z_b[i], w_s[i], b_s[i], w_o[i], ln1_g[i], ln1_b[i],
                   w_ff_a[i], w_ff_b[i], conv_w[i], conv_b[i], w_ff_down[i], ln2_g[i], ln2_b[i],
                   w_ple_gate[i], b_ple_gate[i], w_ple_in[i], ln3_g[i], ln3_b[i],
                   alpha=alpha, tm1=512, ts1=256, tm3=512, ts3=256)
    return x
```

```python
import functools
import math

import numpy as np
import jax
import jax.numpy as jnp
from jax import lax
from jax.experimental import pallas as pl
from jax.experimental.pallas import tpu as pltpu

HEAD_DIM = 64
D_ATTN = 512
D_GMLP = 512
N_GROUPS = 8
CHUNK = 128
ROPE_THETA = 500000.0
ROPE_DIM = 16
LN_EPS = 1e-5
NEG_INF = -1e30
WINDOW_BLOCK = 128
LANES = 128
N_PAIRS = D_ATTN // LANES
DILATIONS = (1, 4, 16)
N_RES = DILATIONS[-1]
CAST_STEPS = 16
LOOKAHEAD = 1
VMEM_LIMIT = 56 * 1024 * 1024

_INV_SQRT2 = 0.7071067811865476


def _gelu(t):
    return 0.5 * t * (1.0 + lax.erf(t * _INV_SQRT2))


def _layernorm_rows(t, g, b):
    mu = jnp.mean(t, axis=-1, keepdims=True)
    d = t - mu
    var = jnp.mean(d * d, axis=-1, keepdims=True)
    return d * lax.rsqrt(var + LN_EPS) * g + b


def _proj_kernel(x_ref, pos_ref, inv_ref, win_ref, lnzg_ref, lnzb_ref, ws_ref, bs_ref, *rest, tm, ts, cast_scales):
    n_cast = len(cast_scales)
    cast_in, rest = rest[:n_cast], rest[n_cast:]
    q_ref, k_ref, v_ref, gm_ref = rest[:4]
    cast_out = rest[4:4 + n_cast]
    wqkv_ref, wuzT_ref, nat_s, mid_s = rest[4 + n_cast:]
    f32, bf = jnp.float32, jnp.bfloat16
    i = pl.program_id(0)

    @pl.when(i == 0)
    def _():
        wqkv_ref[...] = win_ref[:, :3 * D_ATTN].astype(bf)
        wuzT_ref[...] = win_ref[:, 3 * D_ATTN:].T.astype(bf)

    @pl.when(i < CAST_STEPS)
    def _():
        for src, dst, c in zip(cast_in, cast_out, cast_scales):
            dst[...] = (src[...] if c == 1.0 else src[...] * c).astype(bf)

    subs = [(s * ts, ts) for s in range(tm // ts)]
    xb = [x_ref[r0:r0 + n, :].astype(bf) for (r0, n) in subs]
    uz_t = lambda s: lax.dot_general(wuzT_ref[...], xb[s], (((1,), (1,)), ((), ())),
                                     preferred_element_type=f32)
    h_t = {0: uz_t(0)}

    pos = pos_ref[...].astype(f32)
    ang = pos * inv_ref[...]
    cos_t = jnp.cos(ang)
    sin_t = jnp.sin(ang)
    ones48 = jnp.ones((HEAD_DIM - ROPE_DIM, tm), f32)
    zeros48 = jnp.zeros((HEAD_DIM - ROPE_DIM, tm), f32)
    c_tab = jnp.concatenate([cos_t, cos_t, ones48, cos_t, cos_t, ones48], axis=0).T
    s_tab = jnp.concatenate([-sin_t, sin_t, zeros48, -sin_t, sin_t, zeros48], axis=0).T
    lane = lax.broadcasted_iota(jnp.int32, (1, LANES), 1)
    first_half = (lane % HEAD_DIM) < (ROPE_DIM // 2)

    def rope(t, t0, n):
        up = pltpu.roll(t, LANES - ROPE_DIM // 2, 1)
        dn = pltpu.roll(t, ROPE_DIM // 2, 1)
        return t * c_tab[t0:t0 + n] + jnp.where(first_half, up, dn) * s_tab[t0:t0 + n]

    scale = math.log2(math.e) / math.sqrt(HEAD_DIM)

    def qkv_slabs(s, t0, n):
        for j in range(3 * D_ATTN // (2 * LANES)):
            hj = jnp.dot(xb[s], wqkv_ref[:, j * 2 * LANES:(j + 1) * 2 * LANES], preferred_element_type=f32)
            for half in range(2):
                idx = 2 * j + half
                t = hj[:, half * LANES:(half + 1) * LANES]
                if idx < N_PAIRS:
                    t = rope(t, t0, n) * scale
                elif idx < 2 * N_PAIRS:
                    t = rope(t, t0, n)
                nat_s[idx, t0:t0 + n, :] = t

    row = lax.broadcasted_iota(jnp.int32, (CHUNK, CHUNK), 0)
    col = lax.broadcasted_iota(jnp.int32, (CHUNK, CHUNK), 1)
    keep = col <= row
    w_sp = [jnp.where(keep, ws_ref[g], 0.0).astype(bf) for g in range(N_GROUPS)]
    n_chunks = ts // CHUNK
    for s, (t0, n) in enumerate(subs):
        qkv_slabs(s, t0, n)
        if s + 1 < len(subs):
            h_t[s + 1] = uz_t(s + 1)
        u_t = _gelu(h_t[s][:D_GMLP])
        z_t = _gelu(h_t[s][D_GMLP:])
        mu = jnp.mean(z_t, axis=0, keepdims=True)
        d = z_t - mu
        var = jnp.mean(d * d, axis=0, keepdims=True)
        zn_t = (d * lax.rsqrt(var + LN_EPS) * lnzg_ref[...] + lnzb_ref[...]).astype(bf)
        mixed = []
        for g in range(N_GROUPS):
            r0 = g * HEAD_DIM
            lhs = jnp.concatenate([zn_t[r0:r0 + HEAD_DIM, c * CHUNK:(c + 1) * CHUNK] for c in range(n_chunks)],
                                  axis=0)
            res = lax.dot_general(lhs, w_sp[g], (((1,), (1,)), ((), ())),
                                  preferred_element_type=f32) + bs_ref[g:g + 1, :]
            mixed.append(jnp.concatenate([res[c * HEAD_DIM:(c + 1) * HEAD_DIM] for c in range(n_chunks)],
                                         axis=1))
        mixed_t = jnp.concatenate(mixed, axis=0)
        gm_ref[t0:t0 + n, :] = (u_t * mixed_t).T.astype(bf)

    quarter = tm // 4
    n_l = tm // N_RES
    for idx in range(3 * N_PAIRS):
        for r in range(4):
            mid_s[idx, r * quarter:(r + 1) * quarter, :] = nat_s[idx, pl.ds(r, quarter, stride=4), :]
    for which, o_ref in enumerate((q_ref, k_ref, v_ref)):
        for hp in range(N_PAIRS):
            idx = which * N_PAIRS + hp
            for r_lo in range(4):
                for r_hi in range(4):
                    o_ref[hp, 4 * r_hi + r_lo] = mid_s[idx, pl.ds(r_lo * quarter + r_hi, n_l, stride=4), :]


def _branch_masks():
    w = WINDOW_BLOCK
    out = np.zeros((2 * len(DILATIONS), w, 2 * w), np.float32)
    for c, d in enumerate(DILATIONS):
        rows = w * d // N_RES
        aq, i = np.divmod(np.arange(w), rows)
        ak, j = np.divmod(np.arange(2 * w), 2 * rows)
        base = (N_RES // d) * (i[:, None] - j[None, :]) + (aq[:, None] - ak[None, :])
        dist = base + w
        out[2 * c] = (dist >= 0) & (dist <= w)
        dist0 = base
        out[2 * c + 1] = (dist0 >= 0) & (dist0 <= w)
    return out


def _attn_kernel(q_s, k_s, v_s, mask_ref, o_ref, acc_s, m_s, l_s):
    n_l = q_s.shape[1]
    n_spans = n_l // WINDOW_BLOCK

    lane = lax.broadcasted_iota(jnp.int32, (1, LANES), 1)
    head0 = lane < HEAD_DIM

    def gather(ref, chunks):
        return jnp.concatenate([ref[r, pl.ds(st, n), :] for (r, st, n) in chunks], axis=0)

    def run_chains(chains):
        w = WINDOW_BLOCK

        def scores(c):
            q_chunks, kv_chunks = chains[c][:2]
            qv = gather(q_s, q_chunks)
            kb = gather(k_s, kv_chunks).astype(jnp.bfloat16)
            q2 = jnp.concatenate([jnp.where(head0, qv, 0.0), jnp.where(head0, 0.0, qv)],
                                 axis=0).astype(jnp.bfloat16)
            return lax.dot_general(q2, kb, (((1,), (1,)), ((), ())),
                                   preferred_element_type=jnp.float32)

        def finish(c, s2):
            q_chunks, kv_chunks, st_chunks, mask_idx, first, last = chains[c]
            valid = mask_ref[mask_idx] > 0.5
            m_old = None if first else gather(m_s, st_chunks)
            ps, ms = [], []
            for hh in range(2):
                s = jnp.where(valid, s2[hh * w:(hh + 1) * w], NEG_INF)
                t = jnp.maximum(s[:, :LANES], s[:, LANES:])
                if not first:
                    own = head0 if hh == 0 else jnp.logical_not(head0)
                    t = jnp.maximum(t, jnp.where(own, m_old, NEG_INF))
                m_new = jnp.max(t, axis=1, keepdims=True)
                ps.append(jnp.exp2(s - m_new))
                ms.append(m_new)
            p2 = jnp.concatenate(ps, axis=0).astype(jnp.bfloat16)
            m_b = jnp.where(head0, ms[0], ms[1])
            vb = gather(v_s, kv_chunks).astype(jnp.bfloat16)
            vb1 = jnp.concatenate([vb, jnp.ones_like(vb)], axis=1)
            o2 = jnp.dot(p2, vb1, preferred_element_type=jnp.float32)
            o_pair = jnp.where(head0, o2[:w, :LANES], o2[w:, :LANES])
            l_b = jnp.where(head0, o2[:w, LANES:], o2[w:, LANES:])
            if first:
                acc, l_tot = o_pair, l_b
            else:
                alpha = jnp.exp2(m_old - m_b)
                acc = gather(acc_s, st_chunks) * alpha + o_pair
                l_tot = gather(l_s, st_chunks) * alpha + l_b
            if last:
                (r, st, n), = q_chunks
                o_ref[r, pl.ds(st, n), :] = (acc / l_tot).astype(o_ref.dtype)
            else:
                off = 0
                for (r, st, n) in st_chunks:
                    acc_s[r, pl.ds(st, n), :] = acc[off:off + n]
                    m_s[r, pl.ds(st, n), :] = m_b[off:off + n]
                    l_s[r, pl.ds(st, n), :] = l_tot[off:off + n]
                    off += n

        pending = {c: scores(c) for c in range(min(LOOKAHEAD, len(chains)))}
        for c in range(len(chains)):
            s2 = pending.pop(c) if c in pending else scores(c)
            if LOOKAHEAD and c + LOOKAHEAD < len(chains):
                pending[c + LOOKAHEAD] = scores(c + LOOKAHEAD)
            finish(c, s2)

    chains = []
    for span in range(n_spans):
        for c, d in enumerate(DILATIONS):
            rows, per_span = WINDOW_BLOCK * d // N_RES, N_RES // d
            for r_d in range(d):
                pieces = [r_d + d * a for a in range(N_RES // d)]
                for j in range(per_span):
                    n = span * per_span + j
                    q0, k0, s0 = n * rows, max(n - 1, 0) * rows, j * rows
                    chains.append(([(r, q0, rows) for r in pieces], [(r, k0, 2 * rows) for r in pieces],
                                   [(r, s0, rows) for r in pieces], 2 * c + (1 if n == 0 else 0),
                                   c == 0, c == len(DILATIONS) - 1))
    run_chains(chains)


def _tail_kernel(attn_ref, gm_ref, x_ref, p_ref, wo_ref, ln1g_ref, ln1b_ref, wa_ref, wb_ref, cw_ref, cb_ref,
                 wd_ref, ln2g_ref, ln2b_ref, wg_ref, bg_ref, wp_ref, ln3g_ref, ln3b_ref, o_ref, a_s, un_s, um_s,
                 *, tm, ts, tiles_per_seq, alpha):
    i = pl.program_id(0)

    @pl.when(i % tiles_per_seq == 0)
    def _():
        a_s[0:8, :] = jnp.zeros((8, a_s.shape[1]), jnp.float32)

    subs = [(s * ts, ts) for s in range(tm // ts)]
    f32, bf = jnp.float32, jnp.bfloat16
    mm = lambda a, w_ref: jnp.dot(a, w_ref[...], preferred_element_type=f32)

    pe = [mm(p_ref[r0:r0 + n, :].astype(bf), wp_ref) for (r0, n) in subs]

    n_l, quarter = tm // N_RES, tm // 4
    for hp in range(N_PAIRS):
        for r_lo in range(4):
            for r_hi in range(4):
                um_s[hp, pl.ds(r_lo * quarter + r_hi, n_l, stride=4), :] = attn_ref[hp, 4 * r_hi + r_lo].astype(f32)
            un_s[hp, pl.ds(r_lo, quarter, stride=4), :] = um_s[hp, r_lo * quarter:(r_lo + 1) * quarter, :]

    mix = []
    for (r0, n) in subs:
        mixin = jnp.concatenate([un_s[hp, r0:r0 + n, :].astype(bf) for hp in range(N_PAIRS)]
                                + [gm_ref[r0:r0 + n, :]], axis=1)
        mix.append(mm(mixin, wo_ref))
    g1, b1 = ln1g_ref[...] * alpha, ln1b_ref[...] * alpha
    g2, b2 = ln2g_ref[...] * alpha, ln2b_ref[...] * alpha
    cw = cw_ref[...] * _INV_SQRT2
    cb = cb_ref[...] * _INV_SQRT2
    r1 = [_layernorm_rows(alpha * x_ref[r0:r0 + n, :] + mix[s], g1, b1) for s, (r0, n) in enumerate(subs)]
    r1b = [t.astype(bf) for t in r1]
    hb = []
    for s, (r0, n) in enumerate(subs):
        a_s[8 + r0:8 + r0 + n, :] = mm(r1b[s], wa_ref)
        hb.append(mm(r1b[s], wb_ref))
    ff = []
    for s, (r0, n) in enumerate(subs):
        conv = cb + cw[0:1, :] * a_s[6 + r0:6 + r0 + n, :]
        conv = conv + cw[1:2, :] * a_s[7 + r0:7 + r0 + n, :]
        conv = conv + cw[2:3, :] * a_s[8 + r0:8 + r0 + n, :]
        ff.append(mm((conv * (1.0 + lax.erf(conv)) * hb[s]).astype(bf), wd_ref))
    a_s[0:8, :] = a_s[tm:tm + 8, :]
    r2 = [_layernorm_rows(r1[s] + ff[s], g2, b2) for s in range(len(subs))]
    gate = [jax.nn.sigmoid(mm(r2[s].astype(bf), wg_ref) + bg_ref[...]) for s in range(len(subs))]
    for s, (r0, n) in enumerate(subs):
        o_ref[r0:r0 + n, :] = _layernorm_rows(r2[s] + gate[s] * pe[s], ln3g_ref[...], ln3b_ref[...])


def _const_spec(shape):
    nd = len(shape)
    return pl.BlockSpec(shape, lambda *_: (0,) * nd, pipeline_mode=pl.Buffered(1))


def _layer(x, p, positions, w_in, ln_z_g, ln_z_b, w_s, b_s, w_o, ln1_g, ln1_b, w_ff_a, w_ff_b, conv_w, conv_b,
           w_ff_down, ln2_g, ln2_b, w_ple_gate, b_ple_gate, w_ple_in, ln3_g, ln3_b, *, alpha, tm1, ts1, tm3, ts3):
    B, S, D = x.shape
    N = B * S
    F = w_ff_a.shape[1]
    bf = jnp.bfloat16
    xf = x.reshape(N, D)

    inv = np.float32(ROPE_THETA ** (-np.arange(0, ROPE_DIM, 2, dtype=np.float64) / ROPE_DIM)).reshape(8, 1)
    pos3 = positions.reshape(N // tm1, 1, tm1)
    tps1 = S // tm1
    n_l = S // N_RES
    row = lambda v: v.reshape(1, -1)
    later_w = [w_o, w_ff_a, w_ff_b, w_ff_down, w_ple_gate, w_ple_in]
    later_scale = (1.0, 1.0 / alpha, 1.0 / alpha, 0.5 / _INV_SQRT2, 1.0 / alpha, 1.0)
    slab = lambda w: pl.BlockSpec((w.shape[0] // CAST_STEPS, w.shape[1]),
                                  lambda i: (jnp.minimum(i, CAST_STEPS - 1), 0))
    q, k, v, gm, w_o_b, w_a_b, w_b_b, w_d_b, w_g_b, w_p_b = pl.pallas_call(
        functools.partial(_proj_kernel, tm=tm1, ts=ts1, cast_scales=later_scale),
        grid=(N // tm1,),
        in_specs=[
            pl.BlockSpec((tm1, D), lambda i: (i, 0)),
            pl.BlockSpec((None, 1, tm1), lambda i: (i, 0, 0)),
            _const_spec((8, 1)),
            _const_spec(w_in.shape),
            _const_spec((D_GMLP, 1)),
            _const_spec((D_GMLP, 1)),
            _const_spec((N_GROUPS, CHUNK, CHUNK)),
            _const_spec((N_GROUPS, CHUNK)),
        ] + [slab(w) for w in later_w],
        out_specs=[pl.BlockSpec((N_PAIRS, None, N_RES, tm1 // N_RES, LANES),
                                lambda i: (0, i // tps1, 0, i % tps1, 0))] * 3
        + [pl.BlockSpec((tm1, D_GMLP), lambda i: (i, 0))] + [slab(w) for w in later_w],
        out_shape=[jax.ShapeDtypeStruct((N_PAIRS, B, N_RES, n_l, LANES), jnp.float32)] * 3
        + [jax.ShapeDtypeStruct((N, D_GMLP), bf)] + [jax.ShapeDtypeStruct(w.shape, bf) for w in later_w],
        scratch_shapes=[pltpu.VMEM((D, 3 * D_ATTN), bf), pltpu.VMEM((2 * D_GMLP, D), bf)]
        + [pltpu.VMEM((3 * N_PAIRS, tm1, LANES), jnp.float32)] * 2,
        compiler_params=pltpu.CompilerParams(dimension_semantics=("arbitrary",), vmem_limit_bytes=VMEM_LIMIT),
        name="proj",
    )(xf, pos3, jnp.asarray(inv), w_in, ln_z_g.reshape(-1, 1), ln_z_b.reshape(-1, 1), w_s, b_s, *later_w)

    masks = jnp.asarray(_branch_masks())
    seq_spec = pl.BlockSpec((None, None, N_RES, n_l, LANES), lambda b, hp: (hp, b, 0, 0, 0))
    state = pltpu.VMEM((N_RES, WINDOW_BLOCK, LANES), jnp.float32)
    attn = pl.pallas_call(
        _attn_kernel,
        grid=(B, N_PAIRS),
        in_specs=[seq_spec, seq_spec, seq_spec, _const_spec(masks.shape)],
        out_specs=seq_spec,
        out_shape=jax.ShapeDtypeStruct((N_PAIRS, B, N_RES, n_l, LANES), bf),
        scratch_shapes=[state] * 3,
        compiler_params=pltpu.CompilerParams(dimension_semantics=("arbitrary", "arbitrary"),
                                             vmem_limit_bytes=VMEM_LIMIT),
        name="attn",
    )(q, k, v, masks)

    Dp = p.shape[-1]
    tps3 = S // tm3
    out = pl.pallas_call(
        functools.partial(_tail_kernel, tm=tm3, ts=ts3, tiles_per_seq=tps3, alpha=alpha),
        grid=(N // tm3,),
        in_specs=[
            pl.BlockSpec((N_PAIRS, None, N_RES, tm3 // N_RES, LANES), lambda i: (0, i // tps3, 0, i % tps3, 0)),
            pl.BlockSpec((tm3, D_GMLP), lambda i: (i, 0)),
            pl.BlockSpec((tm3, D), lambda i: (i, 0)),
            pl.BlockSpec((tm3, Dp), lambda i: (i, 0)),
            _const_spec((D, D)), _const_spec((1, D)), _const_spec((1, D)),
            _const_spec((D, F)), _const_spec((D, F)), _const_spec((3, F)), _const_spec((1, F)),
            _const_spec((F, D)), _const_spec((1, D)), _const_spec((1, D)),
            _const_spec((D, D)), _const_spec((1, D)), _const_spec((Dp, D)), _const_spec((1, D)), _const_spec((1, D)),
        ],
        out_specs=pl.BlockSpec((tm3, D), lambda i: (i, 0)),
        out_shape=jax.ShapeDtypeStruct((N, D), jnp.float32),
        scratch_shapes=[pltpu.VMEM((tm3 + 8, F), jnp.float32)] + [pltpu.VMEM((N_PAIRS, tm3, LANES), jnp.float32)] * 2,
        compiler_params=pltpu.CompilerParams(dimension_semantics=("arbitrary",), vmem_limit_bytes=VMEM_LIMIT),
        name="tail",
    )(attn, gm, xf, p.reshape(N, Dp), w_o_b, row(ln1_g), row(ln1_b),
      w_a_b, w_b_b, conv_w, row(conv_b), w_d_b, row(ln2_g), row(ln2_b),
      w_g_b, row(b_ple_gate), w_p_b, row(ln3_g), row(ln3_b))
    return out.reshape(B, S, D)


def kernel(x, p, positions, w_in, ln_z_g, ln_z_b, w_s, b_s, w_o, ln1_g, ln1_b, w_ff_a, w_ff_b, conv_w, conv_b,
           w_ff_down, ln2_g, ln2_b, w_ple_gate, b_ple_gate, w_ple_in, ln3_g, ln3_b):
    depth = w_in.shape[0]
    alpha = (2.0 * depth) ** 0.25
    for i in range(depth):
        x = _layer(x, p[i], positions, w_in[i], ln_z_g[i], ln_z_b[i], w_s[i], b_s[i], w_o[i], ln1_g[i], ln1_b[i],
                   w_ff_a[i], w_ff_b[i], conv_w[i], conv_b[i], w_ff_down[i], ln2_g[i], ln2_b[i],
                   w_ple_gate[i], b_ple_gate[i], w_ple_in[i], ln3_g[i], ln3_b[i],
                   alpha=alpha, tm1=512, ts1=256, tm3=512, ts3=256)
    return x
```

```python
import functools
import math

import numpy as np
import jax
import jax.numpy as jnp
from jax import lax
from jax.experimental import pallas as pl
from jax.experimental.pallas import tpu as pltpu

HEAD_DIM = 64
D_ATTN = 512
D_GMLP = 512
N_GROUPS = 8
CHUNK = 128
ROPE_THETA = 500000.0
ROPE_DIM = 16
LN_EPS = 1e-5
NEG_INF = -1e30
WINDOW_BLOCK = 128
LANES = 128
N_PAIRS = D_ATTN // LANES
DILATIONS = (1, 4, 16)
N_RES = DILATIONS[-1]
CAST_STEPS = 16
FF_COLS = 256
LOOKAHEAD = 1
VMEM_LIMIT = 56 * 1024 * 1024

_INV_SQRT2 = 0.7071067811865476


def _gelu(t):
    return 0.5 * t * (1.0 + lax.erf(t * _INV_SQRT2))


def _layernorm_rows(t, g, b):
    mu = jnp.mean(t, axis=-1, keepdims=True)
    d = t - mu
    var = jnp.mean(d * d, axis=-1, keepdims=True)
    return d * lax.rsqrt(var + LN_EPS) * g + b


def _proj_kernel(x_ref, pos_ref, inv_ref, win_ref, lnzg_ref, lnzb_ref, ws_ref, bs_ref, *rest, tm, ts, cast_scales):
    n_cast = len(cast_scales)
    cast_in, rest = rest[:n_cast], rest[n_cast:]
    q_ref, k_ref, v_ref, gm_ref = rest[:4]
    cast_out = rest[4:4 + n_cast]
    wqkv_ref, wuzT_ref, nat_s, mid_s = rest[4 + n_cast:]
    f32, bf = jnp.float32, jnp.bfloat16
    i = pl.program_id(0)

    @pl.when(i == 0)
    def _():
        wqkv_ref[...] = win_ref[:, :3 * D_ATTN].astype(bf)
        wuzT_ref[...] = win_ref[:, 3 * D_ATTN:].T.astype(bf)

    @pl.when(i < CAST_STEPS)
    def _():
        for src, dst, c in zip(cast_in, cast_out, cast_scales):
            dst[...] = (src[...] if c == 1.0 else src[...] * c).astype(bf)

    subs = [(s * ts, ts) for s in range(tm // ts)]
    xb = [x_ref[r0:r0 + n, :].astype(bf) for (r0, n) in subs]
    uz_t = lambda s: lax.dot_general(wuzT_ref[...], xb[s], (((1,), (1,)), ((), ())),
                                     preferred_element_type=f32)
    h_t = {0: uz_t(0)}

    pos = pos_ref[...].astype(f32)
    ang = pos * inv_ref[...]
    cos_t = jnp.cos(ang)
    sin_t = jnp.sin(ang)
    ones48 = jnp.ones((HEAD_DIM - ROPE_DIM, tm), f32)
    zeros48 = jnp.zeros((HEAD_DIM - ROPE_DIM, tm), f32)
    c_tab = jnp.concatenate([cos_t, cos_t, ones48, cos_t, cos_t, ones48], axis=0).T
    s_tab = jnp.concatenate([-sin_t, sin_t, zeros48, -sin_t, sin_t, zeros48], axis=0).T
    lane = lax.broadcasted_iota(jnp.int32, (1, LANES), 1)
    first_half = (lane % HEAD_DIM) < (ROPE_DIM // 2)

    def rope(t, t0, n):
        up = pltpu.roll(t, LANES - ROPE_DIM // 2, 1)
        dn = pltpu.roll(t, ROPE_DIM // 2, 1)
        return t * c_tab[t0:t0 + n] + jnp.where(first_half, up, dn) * s_tab[t0:t0 + n]

    scale = math.log2(math.e) / math.sqrt(HEAD_DIM)

    def qkv_slabs(s, t0, n):
        for j in range(3 * D_ATTN // (2 * LANES)):
            hj = jnp.dot(xb[s], wqkv_ref[:, j * 2 * LANES:(j + 1) * 2 * LANES], preferred_element_type=f32)
            for half in range(2):
                idx = 2 * j + half
                t = hj[:, half * LANES:(half + 1) * LANES]
                if idx < N_PAIRS:
                    t = rope(t, t0, n) * scale
                elif idx < 2 * N_PAIRS:
                    t = rope(t, t0, n)
                nat_s[idx, t0:t0 + n, :] = t

    row = lax.broadcasted_iota(jnp.int32, (CHUNK, CHUNK), 0)
    col = lax.broadcasted_iota(jnp.int32, (CHUNK, CHUNK), 1)
    keep = col <= row
    w_sp = lambda g: jnp.where(keep, ws_ref[g], 0.0).astype(bf)
    n_chunks = ts // CHUNK
    for s, (t0, n) in enumerate(subs):
        qkv_slabs(s, t0, n)
        if s + 1 < len(subs):
            h_t[s + 1] = uz_t(s + 1)
        u_t = _gelu(h_t[s][:D_GMLP])
        z_t = _gelu(h_t[s][D_GMLP:])
        mu = jnp.mean(z_t, axis=0, keepdims=True)
        d = z_t - mu
        var = jnp.mean(d * d, axis=0, keepdims=True)
        zn_t = (d * lax.rsqrt(var + LN_EPS) * lnzg_ref[...] + lnzb_ref[...]).astype(bf)
        mixed = []
        for g in range(N_GROUPS):
            r0 = g * HEAD_DIM
            lhs = jnp.concatenate([zn_t[r0:r0 + HEAD_DIM, c * CHUNK:(c + 1) * CHUNK] for c in range(n_chunks)],
                                  axis=0)
            res = lax.dot_general(lhs, w_sp(g), (((1,), (1,)), ((), ())),
                                  preferred_element_type=f32) + bs_ref[g:g + 1, :]
            mixed.append(jnp.concatenate([res[c * HEAD_DIM:(c + 1) * HEAD_DIM] for c in range(n_chunks)],
                                         axis=1))
        mixed_t = jnp.concatenate(mixed, axis=0)
        gm_ref[t0:t0 + n, :] = (u_t * mixed_t).T.astype(bf)

    quarter = tm // 4
    n_l = tm // N_RES
    for idx in range(3 * N_PAIRS):
        for r in range(4):
            mid_s[idx, r * quarter:(r + 1) * quarter, :] = nat_s[idx, pl.ds(r, quarter, stride=4), :]
    for which, o_ref in enumerate((q_ref, k_ref, v_ref)):
        for hp in range(N_PAIRS):
            idx = which * N_PAIRS + hp
            for r_lo in range(4):
                for r_hi in range(4):
                    o_ref[hp, 4 * r_hi + r_lo] = mid_s[idx, pl.ds(r_lo * quarter + r_hi, n_l, stride=4), :]


def _branch_masks():
    w = WINDOW_BLOCK
    out = np.zeros((2 * len(DILATIONS), w, 2 * w), np.float32)
    for c, d in enumerate(DILATIONS):
        rows = w * d // N_RES
        aq, i = np.divmod(np.arange(w), rows)
        ak, j = np.divmod(np.arange(2 * w), 2 * rows)
        base = (N_RES // d) * (i[:, None] - j[None, :]) + (aq[:, None] - ak[None, :])
        dist = base + w
        out[2 * c] = (dist >= 0) & (dist <= w)
        dist0 = base
        out[2 * c + 1] = (dist0 >= 0) & (dist0 <= w)
    return out


def _attn_kernel(q_s, k_s, v_s, mask_ref, o_ref, acc_s, m_s, l_s):
    n_l = q_s.shape[1]
    n_spans = n_l // WINDOW_BLOCK

    lane = lax.broadcasted_iota(jnp.int32, (1, LANES), 1)
    head0 = lane < HEAD_DIM

    def gather(ref, chunks):
        return jnp.concatenate([ref[r, pl.ds(st, n), :] for (r, st, n) in chunks], axis=0)

    def run_chains(chains):
        w = WINDOW_BLOCK

        def scores(c):
            q_chunks, kv_chunks = chains[c][:2]
            qv = gather(q_s, q_chunks)
            kb = gather(k_s, kv_chunks).astype(jnp.bfloat16)
            q2 = jnp.concatenate([jnp.where(head0, qv, 0.0), jnp.where(head0, 0.0, qv)],
                                 axis=0).astype(jnp.bfloat16)
            return lax.dot_general(q2, kb, (((1,), (1,)), ((), ())),
                                   preferred_element_type=jnp.float32)

        def finish(c, s2):
            q_chunks, kv_chunks, st_chunks, mask_idx, first, last = chains[c]
            valid = mask_ref[mask_idx] > 0.5
            m_old = None if first else gather(m_s, st_chunks)
            ps, ms = [], []
            for hh in range(2):
                s = jnp.where(valid, s2[hh * w:(hh + 1) * w], NEG_INF)
                t = jnp.maximum(s[:, :LANES], s[:, LANES:])
                if not first:
                    own = head0 if hh == 0 else jnp.logical_not(head0)
                    t = jnp.maximum(t, jnp.where(own, m_old, NEG_INF))
                m_new = jnp.max(t, axis=1, keepdims=True)
                ps.append(jnp.exp2(s - m_new))
                ms.append(m_new)
            p2 = jnp.concatenate(ps, axis=0).astype(jnp.bfloat16)
            m_b = jnp.where(head0, ms[0], ms[1])
            vb = gather(v_s, kv_chunks).astype(jnp.bfloat16)
            vb1 = jnp.concatenate([vb, jnp.ones_like(vb)], axis=1)
            o2 = jnp.dot(p2, vb1, preferred_element_type=jnp.float32)
            o_pair = jnp.where(head0, o2[:w, :LANES], o2[w:, :LANES])
            l_b = jnp.where(head0, o2[:w, LANES:], o2[w:, LANES:])
            if first:
                acc, l_tot = o_pair, l_b
            else:
                alpha = jnp.exp2(m_old - m_b)
                acc = gather(acc_s, st_chunks) * alpha + o_pair
                l_tot = gather(l_s, st_chunks) * alpha + l_b
            if last:
                (r, st, n), = q_chunks
                o_ref[r, pl.ds(st, n), :] = (acc / l_tot).astype(o_ref.dtype)
            else:
                off = 0
                for (r, st, n) in st_chunks:
                    acc_s[r, pl.ds(st, n), :] = acc[off:off + n]
                    m_s[r, pl.ds(st, n), :] = m_b[off:off + n]
                    l_s[r, pl.ds(st, n), :] = l_tot[off:off + n]
                    off += n

        pending = {c: scores(c) for c in range(min(LOOKAHEAD, len(chains)))}
        for c in range(len(chains)):
            s2 = pending.pop(c) if c in pending else scores(c)
            if LOOKAHEAD and c + LOOKAHEAD < len(chains):
                pending[c + LOOKAHEAD] = scores(c + LOOKAHEAD)
            finish(c, s2)

    chains = []
    for span in range(n_spans):
        for c, d in enumerate(DILATIONS):
            rows, per_span = WINDOW_BLOCK * d // N_RES, N_RES // d
            for r_d in range(d):
                pieces = [r_d + d * a for a in range(N_RES // d)]
                for j in range(per_span):
                    n = span * per_span + j
                    q0, k0, s0 = n * rows, max(n - 1, 0) * rows, j * rows
                    chains.append(([(r, q0, rows) for r in pieces], [(r, k0, 2 * rows) for r in pieces],
                                   [(r, s0, rows) for r in pieces], 2 * c + (1 if n == 0 else 0),
                                   c == 0, c == len(DILATIONS) - 1))
    run_chains(chains)


def _tail_kernel(attn_ref, gm_ref, x_ref, p_ref, wo_ref, ln1g_ref, ln1b_ref, wa_ref, wb_ref, cw_ref, cb_ref,
                 wd_ref, ln2g_ref, ln2b_ref, wg_ref, bg_ref, wp_ref, ln3g_ref, ln3b_ref, o_ref, a_s, un_s, um_s,
                 *, tm, ts, tiles_per_seq, alpha):
    i = pl.program_id(0)

    @pl.when(i % tiles_per_seq == 0)
    def _():
        a_s[0:8, :] = jnp.zeros((8, a_s.shape[1]), jnp.float32)

    subs = [(s * ts, ts) for s in range(tm // ts)]
    f32, bf = jnp.float32, jnp.bfloat16
    mm = lambda a, w_ref: jnp.dot(a, w_ref[...], preferred_element_type=f32)

    pe = [mm(p_ref[r0:r0 + n, :].astype(bf), wp_ref) for (r0, n) in subs]

    n_l, quarter = tm // N_RES, tm // 4
    for hp in range(N_PAIRS):
        for r_lo in range(4):
            for r_hi in range(4):
                um_s[hp, pl.ds(r_lo * quarter + r_hi, n_l, stride=4), :] = attn_ref[hp, 4 * r_hi + r_lo].astype(f32)
            un_s[hp, pl.ds(r_lo, quarter, stride=4), :] = um_s[hp, r_lo * quarter:(r_lo + 1) * quarter, :]

    mix = []
    for (r0, n) in subs:
        mixin = jnp.concatenate([un_s[hp, r0:r0 + n, :].astype(bf) for hp in range(N_PAIRS)]
                                + [gm_ref[r0:r0 + n, :]], axis=1)
        mix.append(mm(mixin, wo_ref))
    g1, b1 = ln1g_ref[...] * alpha, ln1b_ref[...] * alpha
    g2, b2 = ln2g_ref[...] * alpha, ln2b_ref[...] * alpha
    cw = cw_ref[...] * _INV_SQRT2
    cb = cb_ref[...] * _INV_SQRT2
    r1 = [_layernorm_rows(alpha * x_ref[r0:r0 + n, :] + mix[s], g1, b1) for s, (r0, n) in enumerate(subs)]
    r1b = [t.astype(bf) for t in r1]
    hb = []
    for s, (r0, n) in enumerate(subs):
        a_s[8 + r0:8 + r0 + n, :] = mm(r1b[s], wa_ref)
        hb.append(mm(r1b[s], wb_ref))
    ff = []
    for s, (r0, n) in enumerate(subs):
        gated = []
        for c0 in range(0, a_s.shape[1], FF_COLS):
            cols = slice(c0, c0 + FF_COLS)
            conv = cb[:, cols] + cw[0:1, cols] * a_s[6 + r0:6 + r0 + n, cols]
            conv = conv + cw[1:2, cols] * a_s[7 + r0:7 + r0 + n, cols]
            conv = conv + cw[2:3, cols] * a_s[8 + r0:8 + r0 + n, cols]
            gated.append((conv * (1.0 + lax.erf(conv)) * hb[s][:, cols]).astype(bf))
        ff.append(mm(jnp.concatenate(gated, axis=1), wd_ref))
    a_s[0:8, :] = a_s[tm:tm + 8, :]
    r2 = [_layernorm_rows(r1[s] + ff[s], g2, b2) for s in range(len(subs))]
    gate = [jax.nn.sigmoid(mm(r2[s].astype(bf), wg_ref) + bg_ref[...]) for s in range(len(subs))]
    for s, (r0, n) in enumerate(subs):
        o_ref[r0:r0 + n, :] = _layernorm_rows(r2[s] + gate[s] * pe[s], ln3g_ref[...], ln3b_ref[...])


def _const_spec(shape):
    nd = len(shape)
    return pl.BlockSpec(shape, lambda *_: (0,) * nd, pipeline_mode=pl.Buffered(1))


def _layer(x, p, positions, w_in, ln_z_g, ln_z_b, w_s, b_s, w_o, ln1_g, ln1_b, w_ff_a, w_ff_b, conv_w, conv_b,
           w_ff_down, ln2_g, ln2_b, w_ple_gate, b_ple_gate, w_ple_in, ln3_g, ln3_b, *, alpha, tm1, ts1, tm3, ts3):
    B, S, D = x.shape
    N = B * S
    F = w_ff_a.shape[1]
    bf = jnp.bfloat16
    xf = x.reshape(N, D)

    inv = np.float32(ROPE_THETA ** (-np.arange(0, ROPE_DIM, 2, dtype=np.float64) / ROPE_DIM)).reshape(8, 1)
    pos3 = positions.reshape(N // tm1, 1, tm1)
    tps1 = S // tm1
    n_l = S // N_RES
    row = lambda v: v.reshape(1, -1)
    later_w = [w_o, w_ff_a, w_ff_b, w_ff_down, w_ple_gate, w_ple_in]
    later_scale = (1.0, 1.0 / alpha, 1.0 / alpha, 0.5 / _INV_SQRT2, 1.0 / alpha, 1.0)
    slab = lambda w: pl.BlockSpec((w.shape[0] // CAST_STEPS, w.shape[1]),
                                  lambda i: (jnp.minimum(i, CAST_STEPS - 1), 0))
    q, k, v, gm, w_o_b, w_a_b, w_b_b, w_d_b, w_g_b, w_p_b = pl.pallas_call(
        functools.partial(_proj_kernel, tm=tm1, ts=ts1, cast_scales=later_scale),
        grid=(N // tm1,),
        in_specs=[
            pl.BlockSpec((tm1, D), lambda i: (i, 0)),
            pl.BlockSpec((None, 1, tm1), lambda i: (i, 0, 0)),
            _const_spec((8, 1)),
            _const_spec(w_in.shape),
            _const_spec((D_GMLP, 1)),
            _const_spec((D_GMLP, 1)),
            _const_spec((N_GROUPS, CHUNK, CHUNK)),
            _const_spec((N_GROUPS, CHUNK)),
        ] + [slab(w) for w in later_w],
        out_specs=[pl.BlockSpec((N_PAIRS, None, N_RES, tm1 // N_RES, LANES),
                                lambda i: (0, i // tps1, 0, i % tps1, 0))] * 3
        + [pl.BlockSpec((tm1, D_GMLP), lambda i: (i, 0))] + [slab(w) for w in later_w],
        out_shape=[jax.ShapeDtypeStruct((N_PAIRS, B, N_RES, n_l, LANES), jnp.float32)] * 3
        + [jax.ShapeDtypeStruct((N, D_GMLP), bf)] + [jax.ShapeDtypeStruct(w.shape, bf) for w in later_w],
        scratch_shapes=[pltpu.VMEM((D, 3 * D_ATTN), bf), pltpu.VMEM((2 * D_GMLP, D), bf)]
        + [pltpu.VMEM((3 * N_PAIRS, tm1, LANES), jnp.float32)] * 2,
        compiler_params=pltpu.CompilerParams(dimension_semantics=("arbitrary",), vmem_limit_bytes=VMEM_LIMIT),
        name="proj",
    )(xf, pos3, jnp.asarray(inv), w_in, ln_z_g.reshape(-1, 1), ln_z_b.reshape(-1, 1), w_s, b_s, *later_w)

    masks = jnp.asarray(_branch_masks())
    seq_spec = pl.BlockSpec((None, None, N_RES, n_l, LANES), lambda b, hp: (hp, b, 0, 0, 0))
    state = pltpu.VMEM((N_RES, WINDOW_BLOCK, LANES), jnp.float32)
    attn = pl.pallas_call(
        _attn_kernel,
        grid=(B, N_PAIRS),
        in_specs=[seq_spec, seq_spec, seq_spec, _const_spec(masks.shape)],
        out_specs=seq_spec,
        out_shape=jax.ShapeDtypeStruct((N_PAIRS, B, N_RES, n_l, LANES), bf),
        scratch_shapes=[state] * 3,
        compiler_params=pltpu.CompilerParams(dimension_semantics=("arbitrary", "arbitrary"),
                                             vmem_limit_bytes=VMEM_LIMIT),
        name="attn",
    )(q, k, v, masks)

    Dp = p.shape[-1]
    tps3 = S // tm3
    out = pl.pallas_call(
        functools.partial(_tail_kernel, tm=tm3, ts=ts3, tiles_per_seq=tps3, alpha=alpha),
        grid=(N // tm3,),
        in_specs=[
            pl.BlockSpec((N_PAIRS, None, N_RES, tm3 // N_RES, LANES), lambda i: (0, i // tps3, 0, i % tps3, 0)),
            pl.BlockSpec((tm3, D_GMLP), lambda i: (i, 0)),
            pl.BlockSpec((tm3, D), lambda i: (i, 0)),
            pl.BlockSpec((tm3, Dp), lambda i: (i, 0)),
            _const_spec((D, D)), _const_spec((1, D)), _const_spec((1, D)),
            _const_spec((D, F)), _const_spec((D, F)), _const_spec((3, F)), _const_spec((1, F)),
            _const_spec((F, D)), _const_spec((1, D)), _const_spec((1, D)),
            _const_spec((D, D)), _const_spec((1, D)), _const_spec((Dp, D)), _const_spec((1, D)), _const_spec((1, D)),
        ],
        out_specs=pl.BlockSpec((tm3, D), lambda i: (i, 0)),
        out_shape=jax.ShapeDtypeStruct((N, D), jnp.float32),
        scratch_shapes=[pltpu.VMEM((tm3 + 8, F), jnp.float32)] + [pltpu.VMEM((N_PAIRS, tm3, LANES), jnp.float32)] * 2,
        compiler_params=pltpu.CompilerParams(dimension_semantics=("arbitrary",), vmem_limit_bytes=VMEM_LIMIT),
        name="tail",
    )(attn, gm, xf, p.reshape(N, Dp), w_o_b, row(ln1_g), row(ln1_b),
      w_a_b, w_b_b, conv_w, row(conv_b), w_d_b, row(ln2_g), row(ln2_b),
      w_g_b, row(b_ple_gate), w_p_b, row(ln3_g), row(ln3_b))
    return out.reshape(B, S, D)


def kernel(x, p, positions, w_in, ln_z_g, ln_z_b, w_s, b_s, w_o, ln1_g, ln1_b, w_ff_a, w_ff_b, conv_w, conv_b,
           w_ff_down, ln2_g, ln2_b, w_ple_gate, b_ple_gate, w_ple_in, ln3_g, ln3_b):
    depth = w_in.shape[0]
    alpha = (2.0 * depth) ** 0.25
    for i in range(depth):
        x = _layer(x, p[i], positions, w_in[i], ln_z_g[i], ln_z_b[i], w_s[i], b_s[i], w_o[i], ln1_g[i], ln1_b[i],
                   w_ff_a[i], w_ff_b[i], conv_w[i], conv_b[i], w_ff_down[i], ln2_g[i], ln2_b[i],
                   w_ple_gate[i], b_ple_gate[i], w_ple_in[i], ln3_g[i], ln3_b[i],
                   alpha=alpha, tm1=512, ts1=256, tm3=512, ts3=256)
    return x
```
